```python
import math
import jax, jax.numpy as jnp
from jax import lax
import numpy as np

D_MODEL = 2048
BATCH = 2
SEQ = 16384
DEPTH = 2

N_EVEN = (DEPTH + 1) // 2
N_ODD = DEPTH // 2

POOL_WIDTH = D_MODEL // 2
POOL_WINDOWS = (2, 4, 8, 16)
POOL_GROUP = POOL_WIDTH // len(POOL_WINDOWS)
GMLP_WIDTH = D_MODEL // 2
GMLP_CHUNK = 128
GMLP_GROUPS = 8
GMLP_GROUP_DIM = GMLP_WIDTH // GMLP_GROUPS
IN_EVEN_WIDTH = POOL_WIDTH + 2 * GMLP_WIDTH

DIFF_HEAD_DIM = 64
DIFF_HEADS = D_MODEL // (2 * DIFF_HEAD_DIM)
DIFF_V_DIM = 2 * DIFF_HEAD_DIM
Q_BLOCK = 128
ROPE_THETA = 10000.0

D_FF = 5632
N_EXPERTS = 8
TOP_K = 2
EPS = 1e-5

kernel_name = "hybrid_pool_gmlp_diffattn_moe"


def rms_norm(x, g):
    xf = x.astype(jnp.float32)
    y = xf * lax.rsqrt(jnp.mean(xf * xf, axis=-1, keepdims=True) + EPS)
    return (y * g.astype(jnp.float32)).astype(x.dtype)


def layer_norm(x, g, b):
    xf = x.astype(jnp.float32)
    mu = jnp.mean(xf, axis=-1, keepdims=True)
    xc = xf - mu
    var = jnp.mean(xc * xc, axis=-1, keepdims=True)
    y = xc * lax.rsqrt(var + EPS) * g.astype(jnp.float32) + b.astype(jnp.float32)
    return y.astype(x.dtype)


def swiglu(h, w_gate, w_up, w_down):
    return (jax.nn.silu(h @ w_gate) * (h @ w_up)) @ w_down


def causal_pool_mixer(z, w_pool, pool_scale):
    S_ = z.shape[1]
    zf = z.astype(jnp.float32)
    cnt_base = jnp.arange(S_) + 1
    outs = []
    for g, w in enumerate(POOL_WINDOWS):
        zg = zf[..., g * POOL_GROUP:(g + 1) * POOL_GROUP]
        c = jnp.cumsum(zg, axis=1)
        lower = jnp.pad(c[:, :S_ - w], ((0, 0), (w, 0), (0, 0)))
        cnt = jnp.minimum(cnt_base, w).astype(jnp.float32)[None, :, None]
        pooled = ((c - lower) / cnt - zg).astype(z.dtype)
        outs.append(jnp.einsum('bsc,cd->bsd', pooled, w_pool[g]))
    return jnp.concatenate(outs, axis=-1) * pool_scale


def chunked_spatial_gating(z, ln_g, ln_b, w_spatial, b_spatial):
    a = jax.nn.gelu(z, approximate=False)
    u, v = jnp.split(a, 2, axis=-1)
    v = layer_norm(v, ln_g, ln_b)
    B_, S_, _ = v.shape
    v = v.reshape(B_, S_ // GMLP_CHUNK, GMLP_CHUNK, GMLP_GROUPS, GMLP_GROUP_DIM)
    mask = jnp.tril(jnp.ones((GMLP_CHUNK, GMLP_CHUNK), dtype=bool))
    w = jnp.where(mask[None], w_spatial, 0).astype(v.dtype)
    mixed = jnp.einsum('gts,bnsgc->bntgc', w, v) + b_spatial.T[None, None, :, :, None]
    return u * mixed.reshape(B_, S_, GMLP_WIDTH)


def rotary(x, cos, sin):
    x1, x2 = jnp.split(x, 2, axis=-1)
    return jnp.concatenate([x1 * cos - x2 * sin, x2 * cos + x1 * sin], axis=-1)


def diff_attention(h, positions, w_qkv, lam_q1, lam_k1, lam_q2, lam_k2, subln_g, w_o, lambda_init):
    B_, S_, _ = h.shape
    q, k, v = jnp.split(h @ w_qkv, 3, axis=-1)
    q = q.reshape(B_, S_, DIFF_HEADS, 2, DIFF_HEAD_DIM)
    k = k.reshape(B_, S_, DIFF_HEADS, 2, DIFF_HEAD_DIM)
    v = v.reshape(B_, S_, DIFF_HEADS, DIFF_V_DIM)
    inv_freq = ROPE_THETA ** (-jnp.arange(0, DIFF_HEAD_DIM, 2, dtype=jnp.float32) / DIFF_HEAD_DIM)
    ang = positions.astype(jnp.float32)[..., None] * inv_freq
    cos = jnp.cos(ang)[:, :, None, None, :].astype(h.dtype)
    sin = jnp.sin(ang)[:, :, None, None, :].astype(h.dtype)
    q = rotary(q, cos, sin) * (DIFF_HEAD_DIM ** -0.5)
    k = rotary(k, cos, sin)
    lam = (jnp.exp(jnp.sum((lam_q1 * lam_k1).astype(jnp.float32)))
           - jnp.exp(jnp.sum((lam_q2 * lam_k2).astype(jnp.float32)))
           + lambda_init)
    n_blk = S_ // Q_BLOCK
    q_blocks = q.reshape(B_, n_blk, Q_BLOCK, DIFF_HEADS, 2, DIFF_HEAD_DIM).transpose(1, 0, 2, 3, 4, 5)
    key_pos = jnp.arange(S_)

    def one_block(args):
        qb, i = args
        s = jnp.einsum('bqhcd,bkhcd->bhcqk', qb, k).astype(jnp.float32)
        q_pos = i * Q_BLOCK + jnp.arange(Q_BLOCK)
        causal = key_pos[None, :] <= q_pos[:, None]
        s = jnp.where(causal, s, -jnp.inf)
        p = jax.nn.softmax(s, axis=-1)
        att = p[:, :, 0] - lam * p[:, :, 1]
        return jnp.einsum('bhqk,bkhe->bqhe', att.astype(v.dtype), v)

    o = lax.map(one_block, (q_blocks, jnp.arange(n_blk)))
    o = o.transpose(1, 0, 2, 3, 4).reshape(B_, S_, DIFF_HEADS, DIFF_V_DIM)
    o = rms_norm(o, subln_g) * (1.0 - lambda_init)
    return o.reshape(B_, S_, DIFF_HEADS * DIFF_V_DIM) @ w_o


def moe_swiglu(h, w_router, we_gate, we_up, we_down):
    logits = (h @ w_router).astype(jnp.float32)
    top_v, top_i = lax.top_k(logits, TOP_K)
    top_w = jax.nn.softmax(top_v, axis=-1)
    gates = jnp.sum(jax.nn.one_hot(top_i, N_EXPERTS, dtype=jnp.float32) * top_w[..., None], axis=-2)
    y = jnp.zeros_like(h)
    for e in range(N_EXPERTS):
        y = y + gates[..., e:e + 1].astype(h.dtype) * swiglu(h, we_gate[e], we_up[e], we_down[e])
    return y


def setup_inputs(seed: int = 0) -> dict:
    key = jax.random.key(seed)
    ks = iter(jax.random.split(key, 40))
    f32 = jnp.float32

    def nrm(shape, scale):
        return jax.random.normal(next(ks), shape, f32) * scale

    def gain(shape):
        return 1.0 + 0.05 * jax.random.normal(next(ks), shape, f32)

    D = D_MODEL
    return {
        "x": jax.random.normal(next(ks), (BATCH, SEQ, D), f32),
        "positions": jnp.tile(jnp.arange(SEQ, dtype=jnp.int32)[None], (BATCH, 1)),
        "ev_norm_mix": gain((N_EVEN, D)),
        "ev_w_in": nrm((N_EVEN, D, IN_EVEN_WIDTH), D ** -0.5),
        "ev_w_pool": nrm((N_EVEN, len(POOL_WINDOWS), POOL_GROUP, POOL_GROUP), POOL_GROUP ** -0.5),
        "ev_pool_scale": gain((N_EVEN, POOL_WIDTH)),
        "ev_ln_g": gain((N_EVEN, GMLP_WIDTH)),
        "ev_ln_b": nrm((N_EVEN, GMLP_WIDTH), 0.02),
        "ev_w_spatial": nrm((N_EVEN, GMLP_GROUPS, GMLP_CHUNK, GMLP_CHUNK), GMLP_CHUNK ** -0.5),
        "ev_b_spatial": 1.0 + nrm((N_EVEN, GMLP_GROUPS, GMLP_CHUNK), 0.02),
        "ev_w_out": nrm((N_EVEN, POOL_WIDTH + GMLP_WIDTH, D), (POOL_WIDTH + GMLP_WIDTH) ** -0.5),
        "ev_norm_ffn": gain((N_EVEN, D)),
        "ev_w_gate": nrm((N_EVEN, D, D_FF), D ** -0.5),
        "ev_w_up": nrm((N_EVEN, D, D_FF), D ** -0.5),
        "ev_w_down": nrm((N_EVEN, D_FF, D), D_FF ** -0.5),
        "od_norm_attn": gain((N_ODD, D)),
        "od_w_qkv": nrm((N_ODD, D, 3 * D), D ** -0.5),
        "od_lam_q1": nrm((N_ODD, DIFF_HEAD_DIM), 0.1),
        "od_lam_k1": nrm((N_ODD, DIFF_HEAD_DIM), 0.1),
        "od_lam_q2": nrm((N_ODD, DIFF_HEAD_DIM), 0.1),
        "od_lam_k2": nrm((N_ODD, DIFF_HEAD_DIM), 0.1),
        "od_subln_g": gain((N_ODD, DIFF_V_DIM)),
        "od_w_o": nrm((N_ODD, D, D), D ** -0.5),
        "od_norm_moe": gain((N_ODD, D)),
        "od_w_router": nrm((N_ODD, D, N_EXPERTS), D ** -0.5),
        "od_we_gate": nrm((N_ODD, N_EXPERTS, D, D_FF), D ** -0.5),
        "od_we_up": nrm((N_ODD, N_EXPERTS, D, D_FF), D ** -0.5),
        "od_we_down": nrm((N_ODD, N_EXPERTS, D_FF, D), D_FF ** -0.5),
        "final_norm": gain((D,)),
    }


def reference(x, positions,
              ev_norm_mix, ev_w_in, ev_w_pool, ev_pool_scale, ev_ln_g, ev_ln_b,
              ev_w_spatial, ev_b_spatial, ev_w_out, ev_norm_ffn, ev_w_gate, ev_w_up, ev_w_down,
              od_norm_attn, od_w_qkv, od_lam_q1, od_lam_k1, od_lam_q2, od_lam_k2, od_subln_g,
              od_w_o, od_norm_moe, od_w_router, od_we_gate, od_we_up, od_we_down,
              final_norm):
    for i in range(DEPTH):
        j = i // 2
        if i % 2 == 0:
            h = rms_norm(x, ev_norm_mix[j])
            z = h @ ev_w_in[j]
            y_pool = causal_pool_mixer(z[..., :POOL_WIDTH], ev_w_pool[j], ev_pool_scale[j])
            y_gate = chunked_spatial_gating(z[..., POOL_WIDTH:], ev_ln_g[j], ev_ln_b[j],
                                            ev_w_spatial[j], ev_b_spatial[j])
            x = x + jnp.concatenate([y_pool, y_gate], axis=-1) @ ev_w_out[j]
            x = x + swiglu(rms_norm(x, ev_norm_ffn[j]), ev_w_gate[j], ev_w_up[j], ev_w_down[j])
        else:
            lambda_init = 0.8 - 0.6 * math.exp(-0.3 * i)
            h = rms_norm(x, od_norm_attn[j])
            x = x + diff_attention(h, positions, od_w_qkv[j], od_lam_q1[j], od_lam_k1[j],
                                   od_lam_q2[j], od_lam_k2[j], od_subln_g[j], od_w_o[j], lambda_init)
            x = x + moe_swiglu(rms_norm(x, od_norm_moe[j]), od_w_router[j],
                               od_we_gate[j], od_we_up[j], od_we_down[j])
    return rms_norm(x, final_norm)
```

```python
import functools
import math

import jax
import jax.numpy as jnp
from jax import lax
from jax.experimental import pallas as pl
from jax.experimental.pallas import tpu as pltpu

EPS = 1e-5
POOL_WINDOWS = (2, 4, 8, 16)
POOL_HALO = 16
GMLP_CHUNK = 128
GMLP_GROUPS = 8
DIFF_HEAD_DIM = 64
HEAD_WIDTH = 2 * DIFF_HEAD_DIM
ROPE_THETA = 10000.0
N_EXPERTS = 8
TOP_K = 2
LANES = 128
NEG_BIG = -1e30

VMEM_LIMIT_BYTES = 60 * 1024 * 1024

TM_MIX = 256
TM_FFN = 512
TF_FFN = 512
TM_PROJ = 512
TQ_ATT = 512
TK_ATT = 512
TM_OUT = 256
TM_MOE = 512
TM_ROW = 256


def _cparams(sem):
    return pltpu.CompilerParams(dimension_semantics=sem, vmem_limit_bytes=VMEM_LIMIT_BYTES)


def _const_spec(shape):
    nd = len(shape)
    return pl.BlockSpec(shape, lambda *_: (0,) * nd, pipeline_mode=pl.Buffered(1))


def _rms(x, g):
    return x * lax.rsqrt(jnp.mean(x * x, axis=-1, keepdims=True) + EPS) * g


def _bdot(a, b):
    return jnp.dot(a, b, preferred_element_type=jnp.float32)


def _mixer_body(tiles_per_seq, x_ref, g_ref, w_in_ref, w_pool_ref, pscale_ref, ln_g_ref, ln_b_ref,
                w_sp_ref, b_sp_ref, w_out_ref, g_next_ref, x_out_ref, h_out_ref, ext_ref):
    tm = x_ref.shape[0]
    pool_w = w_pool_ref.shape[0] * w_pool_ref.shape[1]
    pool_g = w_pool_ref.shape[1]
    gm_w = ln_g_ref.shape[1]
    i = pl.program_id(0)
    seq_tile = i % tiles_per_seq

    x = x_ref[...]
    h = _rms(x, g_ref[...]).astype(jnp.bfloat16)
    z = _bdot(h, w_in_ref[...])

    zp = z[:, :pool_w]

    @pl.when(seq_tile == 0)
    def _():
        ext_ref[0:POOL_HALO, :] = jnp.zeros((POOL_HALO, pool_w), jnp.float32)

    ext_ref[POOL_HALO:, :] = zp
    tpos = seq_tile * tm + lax.broadcasted_iota(jnp.int32, (tm, 1), 0)
    pooled_parts = []
    for gi, w in enumerate(POOL_WINDOWS):
        cols = slice(gi * pool_g, (gi + 1) * pool_g)
        p = ext_ref[:, cols]
        k = 1
        while k < w:
            p = p + pltpu.roll(p, k, axis=0)
            k *= 2
        cnt = jnp.minimum(tpos + 1, w).astype(jnp.float32)
        pooled = p[POOL_HALO:, :] / cnt - zp[:, cols]
        pooled_parts.append(_bdot(pooled.astype(jnp.bfloat16), w_pool_ref[gi]))
    ext_ref[0:POOL_HALO, :] = zp[tm - POOL_HALO:, :]
    y_pool = jnp.concatenate(pooled_parts, axis=-1) * pscale_ref[...]

    a = z[:, pool_w:]
    a = 0.5 * a * (1.0 + lax.erf(a * (1.0 / math.sqrt(2.0))))
    u = a[:, :gm_w]
    v = a[:, gm_w:]
    mu = jnp.mean(v, axis=-1, keepdims=True)
    vc = v - mu
    var = jnp.mean(vc * vc, axis=-1, keepdims=True)
    vn = (vc * lax.rsqrt(var + EPS) * ln_g_ref[...] + ln_b_ref[...]).astype(jnp.bfloat16)
    gd = gm_w // GMLP_GROUPS
    row = lax.broadcasted_iota(jnp.int32, (GMLP_CHUNK, GMLP_CHUNK), 0)
    col = lax.broadcasted_iota(jnp.int32, (GMLP_CHUNK, GMLP_CHUNK), 1)
    tril = col <= row
    w_sp = [jnp.where(tril, w_sp_ref[g], 0.0).astype(jnp.bfloat16) for g in range(GMLP_GROUPS)]
    b_sp = b_sp_ref[...]
    rows_out = []
    for c in range(tm // GMLP_CHUNK):
        rs = slice(c * GMLP_CHUNK, (c + 1) * GMLP_CHUNK)
        mixed = jnp.concatenate(
            [_bdot(w_sp[g], vn[rs, g * gd:(g + 1) * gd]) for g in range(GMLP_GROUPS)], axis=-1)
        rows_out.append(u[rs, :] * (mixed + b_sp))
    y_gate = jnp.concatenate(rows_out, axis=0)

    y = jnp.concatenate([y_pool, y_gate], axis=-1).astype(jnp.bfloat16)
    x1 = x + _bdot(y, w_out_ref[...])
    x_out_ref[...] = x1
    h_out_ref[...] = _rms(x1, g_next_ref[...]).astype(jnp.bfloat16)


def _mixer(x, seq, g, w_in, w_pool, pscale, ln_g, ln_b, w_sp, b_sp_full, w_out, g_next):
    n, d = x.shape
    tm = TM_MIX
    assert seq % tm == 0 and tm % GMLP_CHUNK == 0 and tm >= POOL_HALO
    pool_w = w_pool.shape[0] * w_pool.shape[1]
    row_spec = pl.BlockSpec((tm, d), lambda i: (i, 0))
    return pl.pallas_call(
        functools.partial(_mixer_body, seq // tm),
        grid=(n // tm,),
        in_specs=[row_spec, _const_spec(g.shape), _const_spec(w_in.shape), _const_spec(w_pool.shape),
                  _const_spec(pscale.shape), _const_spec(ln_g.shape), _const_spec(ln_b.shape),
                  _const_spec(w_sp.shape), _const_spec(b_sp_full.shape), _const_spec(w_out.shape),
                  _const_spec(g_next.shape)],
        out_specs=[row_spec, row_spec],
        out_shape=[jax.ShapeDtypeStruct((n, d), jnp.float32), jax.ShapeDtypeStruct((n, d), jnp.bfloat16)],
        scratch_shapes=[pltpu.VMEM((tm + POOL_HALO, pool_w), jnp.float32)],
        compiler_params=_cparams(("arbitrary",)),
        name="mixer0",
    )(x, g, w_in, w_pool, pscale, ln_g, ln_b, w_sp, b_sp_full, w_out, g_next)


def _swiglu_step(h, wg_ref, wu_ref, wd_ref, acc_ref):
    a = _bdot(h, wg_ref[...])
    b = _bdot(h, wu_ref[...])
    y = (a * (1.0 / (1.0 + jnp.exp(-a))) * b).astype(jnp.bfloat16)
    acc_ref[...] += _bdot(y, wd_ref[...])


def _dense_ffn_body(te_ref, x_ref, h_ref, wg_ref, wu_ref, wd_ref, g_next_ref, x_out_ref, h_out_ref, acc_ref):
    j = pl.program_id(1)

    @pl.when(j == 0)
    def _():
        acc_ref[...] = jnp.zeros_like(acc_ref)

    _swiglu_step(h_ref[...], wg_ref, wu_ref, wd_ref, acc_ref)

    @pl.when(j == pl.num_programs(1) - 1)
    def _():
        x2 = x_ref[...] + acc_ref[...]
        x_out_ref[...] = x2
        h_out_ref[...] = _rms(x2, g_next_ref[...]).astype(jnp.bfloat16)


def _moe_ffn_body(te_ref, nv_ref, x_ref, g_ref, wg_ref, wu_ref, wd_ref, y_out_ref, h_ref, acc_ref):
    i = pl.program_id(0)
    j = pl.program_id(1)

    @pl.when(i < nv_ref[0])
    def _():
        @pl.when(j == 0)
        def _():
            acc_ref[...] = jnp.zeros_like(acc_ref)
            h_ref[...] = _rms(x_ref[...], g_ref[...]).astype(jnp.bfloat16)

        _swiglu_step(h_ref[...], wg_ref, wu_ref, wd_ref, acc_ref)

        @pl.when(j == pl.num_programs(1) - 1)
        def _():
            y_out_ref[...] = acc_ref[...]

    @pl.when((i >= nv_ref[0]) & (j == 0))
    def _():
        y_out_ref[...] = jnp.zeros_like(y_out_ref)


def _weight_specs(d, ff, tf, expert_of):
    wg_spec = pl.BlockSpec((None, d, tf), lambda i, j, *s: (expert_of(i, *s), 0, j))
    wd_spec = pl.BlockSpec((None, tf, d), lambda i, j, *s: (expert_of(i, *s), j, 0))
    return wg_spec, wg_spec, wd_spec


def _dense_ffn(x, h, wg, wu, wd, g_next):
    n, d = x.shape
    ff = wg.shape[-1]
    tm, tf = TM_FFN, TF_FFN
    assert n % tm == 0 and ff % tf == 0
    row_spec = pl.BlockSpec((tm, d), lambda i, j, *s: (i, 0))
    zero = jnp.zeros((1,), jnp.int32)
    return pl.pallas_call(
        _dense_ffn_body,
        grid_spec=pltpu.PrefetchScalarGridSpec(
            num_scalar_prefetch=1,
            grid=(n // tm, ff // tf),
            in_specs=[row_spec, row_spec, *_weight_specs(d, ff, tf, lambda i, te: te[0]),
                      pl.BlockSpec(g_next.shape, lambda i, j, *s: (0, 0))],
            out_specs=[row_spec, row_spec],
            scratch_shapes=[pltpu.VMEM((tm, d), jnp.float32)],
        ),
        out_shape=[jax.ShapeDtypeStruct((n, d), jnp.float32), jax.ShapeDtypeStruct((n, d), jnp.bfloat16)],
        compiler_params=_cparams(("arbitrary", "arbitrary")),
        name="dense_ffn",
    )(zero, x, h, wg, wu, wd, g_next)


def _moe_ffn(tile_expert, n_valid, xs, g, wg, wu, wd):
    r, d = xs.shape
    ff = wg.shape[-1]
    tm, tf = TM_MOE, TF_FFN
    assert r % tm == 0 and ff % tf == 0
    n_tiles = r // tm

    def row_idx(i, j, te, nv):
        return (jnp.minimum(i, nv[0] - 1), 0)

    def expert_of(i, te, nv):
        return te[jnp.minimum(i, nv[0] - 1)]

    def ff_idx(i, j, nv):
        return jnp.where(i < nv[0], j, ff // tf - 1)

    wg_spec = pl.BlockSpec((None, d, tf), lambda i, j, te, nv: (expert_of(i, te, nv), 0, ff_idx(i, j, nv)))
    wd_spec = pl.BlockSpec((None, tf, d), lambda i, j, te, nv: (expert_of(i, te, nv), ff_idx(i, j, nv), 0))
    row_spec = pl.BlockSpec((tm, d), row_idx)
    return pl.pallas_call(
        _moe_ffn_body,
        grid_spec=pltpu.PrefetchScalarGridSpec(
            num_scalar_prefetch=2,
            grid=(n_tiles, ff // tf),
            in_specs=[row_spec, pl.BlockSpec(g.shape, lambda i, j, *s: (0, 0)), wg_spec, wg_spec, wd_spec],
            out_specs=pl.BlockSpec((tm, d), lambda i, j, *s: (i, 0)),
            scratch_shapes=[pltpu.VMEM((tm, d), jnp.bfloat16), pltpu.VMEM((tm, d), jnp.float32)],
        ),
        out_shape=jax.ShapeDtypeStruct((r, d), jnp.float32),
        compiler_params=_cparams(("arbitrary", "arbitrary")),
        name="moe_ffn",
    )(tile_expert, n_valid, xs, g, wg, wu, wd)


def _proj_rope_body(scale, h_ref, w_ref, pos_ref, invf_ref, sign_ref, o_ref):
    t = _bdot(h_ref[...], w_ref[...])
    half = DIFF_HEAD_DIM // 2
    ang = pos_ref[...].astype(jnp.float32) * invf_ref[...]
    cos = jnp.cos(ang) * scale
    sin = jnp.sin(ang) * sign_ref[...] * scale
    lane = lax.broadcasted_iota(jnp.int32, (1, LANES), 1)
    first_half = (lane % DIFF_HEAD_DIM) < half
    for hd in range(o_ref.shape[0]):
        blk = t[:, hd * HEAD_WIDTH:(hd + 1) * HEAD_WIDTH]
        partner = jnp.where(first_half, pltpu.roll(blk, LANES - half, axis=1), pltpu.roll(blk, half, axis=1))
        o_ref[hd] = (blk * cos + partner * sin).astype(o_ref.dtype)


def _proj_rope(h, w, pos, invf, sign, batch, scale):
    n, d = h.shape
    seq = n // batch
    tm = TM_PROJ
    heads = d // HEAD_WIDTH
    tiles = seq // tm
    return pl.pallas_call(
        functools.partial(_proj_rope_body, scale),
        grid=(n // tm,),
        in_specs=[pl.BlockSpec((tm, d), lambda i: (i, 0)), _const_spec(w.shape),
                  pl.BlockSpec((tm, 1), lambda i: (i, 0)), _const_spec(invf.shape), _const_spec(sign.shape)],
        out_specs=pl.BlockSpec((None, heads, tm, HEAD_WIDTH), lambda i: (i // tiles, 0, i % tiles, 0)),
        out_shape=jax.ShapeDtypeStruct((batch, heads, seq, HEAD_WIDTH), jnp.bfloat16),
        compiler_params=_cparams(("parallel",)),
        name="proj_rope",
    )(h, w, pos, invf, sign)


def _proj_vt_body(h_ref, w_ref, o_ref):
    t = _bdot(h_ref[...], w_ref[...])
    for hd in range(o_ref.shape[0]):
        o_ref[hd, 0] = t[:, hd * HEAD_WIDTH:(hd + 1) * HEAD_WIDTH].T.astype(o_ref.dtype)


def _proj_vt(h, w, batch):
    n, d = h.shape
    seq = n // batch
    tm = TK_ATT
    heads = d // HEAD_WIDTH
    tiles = seq // tm
    return pl.pallas_call(
        _proj_vt_body,
        grid=(n // tm,),
        in_specs=[pl.BlockSpec((tm, d), lambda i: (i, 0)), _const_spec(w.shape)],
        out_specs=pl.BlockSpec((None, heads, 1, HEAD_WIDTH, tm), lambda i: (i // tiles, 0, i % tiles, 0, 0)),
        out_shape=jax.ShapeDtypeStruct((batch, heads, tiles, HEAD_WIDTH, tm), jnp.bfloat16),
        compiler_params=_cparams(("parallel",)),
        name="proj_vt",
    )(h, w)


def _attn_body(lam_init, lam_ref, q_ref, k_ref, vt_ref, g_ref, o_ref, m_ref, l_ref, acc_ref):
    tq = q_ref.shape[0]
    tk = vt_ref.shape[2]
    qi = pl.program_id(2)
    lane = lax.broadcasted_iota(jnp.int32, (1, HEAD_WIDTH), 1)
    q = q_ref[...]
    zero = jnp.zeros_like(q)
    qz = (jnp.where(lane < DIFF_HEAD_DIM, q, zero), jnp.where(lane >= DIFF_HEAD_DIM, q, zero))

    m_ref[...] = jnp.full_like(m_ref, NEG_BIG)
    l_ref[...] = jnp.zeros_like(l_ref)
    acc_ref[...] = jnp.zeros_like(acc_ref)

    def block(kv, masked):
        start = pl.multiple_of(kv * tk, tk)
        kb = k_ref[pl.ds(start, tk), :]
        vtb = vt_ref[kv]
        for c in range(2):
            s = lax.dot_general(kb, qz[c], (((1,), (1,)), ((), ())), preferred_element_type=jnp.float32)
            if masked:
                krow = lax.broadcasted_iota(jnp.int32, (tk, tq), 0)
                qcol = lax.broadcasted_iota(jnp.int32, (tk, tq), 1)
                s = jnp.where(krow <= qcol, s, NEG_BIG)
            m_old = m_ref[c]
            m_new = jnp.maximum(m_old, jnp.max(s, axis=0, keepdims=True))
            alpha = jnp.exp(m_old - m_new)
            p = jnp.exp(s - m_new)
            l_ref[c] = alpha * l_ref[c] + jnp.sum(p, axis=0, keepdims=True)
            acc_ref[c] = alpha * acc_ref[c] + _bdot(vtb, p.astype(jnp.bfloat16))
            m_ref[c] = m_new

    def body(kv, carry):
        block(kv, False)
        return carry

    lax.fori_loop(0, qi, body, 0)
    block(qi, True)

    o = acc_ref[0] / l_ref[0] - lam_ref[0] * (acc_ref[1] / l_ref[1])
    o = o * lax.rsqrt(jnp.mean(o * o, axis=0, keepdims=True) + EPS) * g_ref[...] * (1.0 - lam_init)
    o_ref[...] = o.T.astype(o_ref.dtype)


def _attention(lam, q, k, vt, g_col, lam_init):
    batch, heads, seq, hw = q.shape
    tq, tk = TQ_ATT, TK_ATT
    assert tq == tk and seq % tq == 0
    return pl.pallas_call(
        functools.partial(_attn_body, lam_init),
        grid=(batch, heads, seq // tq),
        in_specs=[pl.BlockSpec(memory_space=pltpu.SMEM),
                  pl.BlockSpec((None, None, tq, hw), lambda b, h, i: (b, h, i, 0)),
                  pl.BlockSpec((None, None, seq, hw), lambda b, h, i: (b, h, 0, 0)),
                  pl.BlockSpec((None, None, seq // tk, hw, tk), lambda b, h, i: (b, h, 0, 0, 0)),
                  pl.BlockSpec(g_col.shape, lambda b, h, i: (0, 0))],
        out_specs=pl.BlockSpec((None, None, tq, hw), lambda b, h, i: (b, h, i, 0)),
        out_shape=jax.ShapeDtypeStruct((batch, heads, seq, hw), jnp.bfloat16),
        scratch_shapes=[pltpu.VMEM((2, 1, tq), jnp.float32), pltpu.VMEM((2, 1, tq), jnp.float32),
                        pltpu.VMEM((2, hw, tq), jnp.float32)],
        compiler_params=_cparams(("parallel", "parallel", "arbitrary")),
        name="diff_attn",
    )(lam, q, k, vt, g_col)


ROUTE_IDX, ROUTE_GATE, ROUTE_RANK = 0, 2, 4


def _out_router_body(o_ref, x_ref, w_o_ref, g_ref, w_r_ref, x_out_ref, route_ref, count_ref, run_ref):
    heads, tm, _ = o_ref.shape

    @pl.when((pl.program_id(0) == 0) & (pl.program_id(1) == 0))
    def _():
        run_ref[...] = jnp.zeros_like(run_ref)

    o = jnp.concatenate([o_ref[hd] for hd in range(heads)], axis=-1)
    x3 = x_ref[...] + _bdot(o, w_o_ref[...])
    x_out_ref[...] = x3

    h = _rms(x3, g_ref[...])
    logits = jnp.dot(h, w_r_ref[...], precision=lax.Precision.HIGHEST, preferred_element_type=jnp.float32)
    lane = lax.broadcasted_iota(jnp.int32, (tm, LANES), 1).astype(jnp.float32)
    logits = jnp.where(lane < N_EXPERTS, logits, -jnp.inf)
    m1 = jnp.max(logits, axis=-1, keepdims=True)
    e1 = jnp.min(jnp.where(logits == m1, lane, float(LANES)), axis=-1, keepdims=True)
    sel1 = lane == e1
    rest = jnp.where(sel1, -jnp.inf, logits)
    m2 = jnp.max(rest, axis=-1, keepdims=True)
    e2 = jnp.min(jnp.where(rest == m2, lane, float(LANES)), axis=-1, keepdims=True)
    sel2 = lane == e2
    t = jnp.exp(m2 - m1)
    w1 = 1.0 / (1.0 + t)
    w2 = t / (1.0 + t)

    onehot = (sel1 | sel2).astype(jnp.bfloat16)
    r_i = lax.broadcasted_iota(jnp.int32, (tm, tm), 0)
    c_i = lax.broadcasted_iota(jnp.int32, (tm, tm), 1)
    lower = (c_i < r_i).astype(jnp.bfloat16)
    before = _bdot(lower, onehot) + run_ref[...]
    run_ref[...] += jnp.sum(onehot.astype(jnp.float32), axis=0, keepdims=True)
    rank1 = jnp.sum(jnp.where(sel1, before, 0.0), axis=-1, keepdims=True)
    rank2 = jnp.sum(jnp.where(sel2, before, 0.0), axis=-1, keepdims=True)

    route = jnp.zeros((tm, LANES), jnp.float32)
    for ln, val in ((ROUTE_IDX, e1), (ROUTE_IDX + 1, e2), (ROUTE_GATE, w1), (ROUTE_GATE + 1, w2),
                    (ROUTE_RANK, rank1), (ROUTE_RANK + 1, rank2)):
        route = jnp.where(lane == float(ln), val, route)
    route_ref[...] = route
    count_ref[...] = run_ref[...]


def _out_router(o, x, w_o, g, w_r_pad):
    batch, heads, seq, hw = o.shape
    d = heads * hw
    tm = TM_OUT
    tiles = seq // tm
    n = batch * seq
    return pl.pallas_call(
        _out_router_body,
        grid=(batch, tiles),
        in_specs=[pl.BlockSpec((None, heads, tm, hw), lambda b, i: (b, 0, i, 0)),
                  pl.BlockSpec((tm, d), lambda b, i: (b * tiles + i, 0)),
                  _const_spec(w_o.shape), _const_spec(g.shape), _const_spec(w_r_pad.shape)],
        out_specs=[pl.BlockSpec((tm, d), lambda b, i: (b * tiles + i, 0)),
                   pl.BlockSpec((tm, LANES), lambda b, i: (b * tiles + i, 0)),
                   pl.BlockSpec((1, LANES), lambda b, i: (0, 0))],
        out_shape=[jax.ShapeDtypeStruct((n, d), jnp.float32), jax.ShapeDtypeStruct((n, LANES), jnp.float32),
                   jax.ShapeDtypeStruct((1, LANES), jnp.float32)],
        scratch_shapes=[pltpu.VMEM((1, LANES), jnp.float32)],
        compiler_params=_cparams(("arbitrary", "arbitrary")),
        name="out_router",
    )(o, x, w_o, g, w_r_pad)


def _row_copy(src_ref, dst_ref, sem):
    return pltpu.make_async_copy(src_ref, dst_ref, sem)


def _dispatch_body(dest_ref, x_ref, xs_in_ref, xs_ref, sem):
    del xs_in_ref
    tm = x_ref.shape[0]

    def issue(r, c):
        for k in range(TOP_K):
            _row_copy(x_ref.at[pl.ds(r, 1)], xs_ref.at[pl.ds(dest_ref[0, k * tm + r], 1)], sem).start()
        return c

    lax.fori_loop(0, tm, issue, 0)

    def drain(r, c):
        for k in range(TOP_K):
            _row_copy(x_ref.at[pl.ds(0, 1)], xs_ref.at[pl.ds(0, 1)], sem).wait()
        return c

    lax.fori_loop(0, tm, drain, 0)


def _dispatch(dest3, x, rows):
    n, d = x.shape
    tm = TM_ROW
    zeros = jnp.zeros((rows, d), x.dtype)
    return pl.pallas_call(
        _dispatch_body,
        grid=(n // tm,),
        in_specs=[pl.BlockSpec((None, 1, TOP_K * tm), lambda i: (i, 0, 0), memory_space=pltpu.SMEM),
                  pl.BlockSpec((tm, d), lambda i: (i, 0)),
                  pl.BlockSpec(memory_space=pl.ANY)],
        out_specs=pl.BlockSpec(memory_space=pl.ANY),
        out_shape=jax.ShapeDtypeStruct((rows, d), x.dtype),
        scratch_shapes=[pltpu.SemaphoreType.DMA],
        input_output_aliases={2: 0},
        compiler_params=_cparams(("arbitrary",)),
        name="dispatch",
    )(dest3, x, zeros)


def _combine_body(dest_ref, x_ref, route_ref, g_ref, ys_ref, o_ref, buf_ref, sem):
    tm = x_ref.shape[0]

    def issue(r, c):
        for k in range(TOP_K):
            _row_copy(ys_ref.at[pl.ds(dest_ref[0, k * tm + r], 1)], buf_ref.at[k, pl.ds(r, 1)], sem).start()
        return c

    lax.fori_loop(0, tm, issue, 0)

    def drain(r, c):
        for k in range(TOP_K):
            _row_copy(ys_ref.at[pl.ds(0, 1)], buf_ref.at[k, pl.ds(0, 1)], sem).wait()
        return c

    lax.fori_loop(0, tm, drain, 0)

    route = route_ref[...]
    w1 = route[:, ROUTE_GATE:ROUTE_GATE + 1]
    w2 = route[:, ROUTE_GATE + 1:ROUTE_GATE + 2]
    y = x_ref[...] + w1 * buf_ref[0] + w2 * buf_ref[1]
    o_ref[...] = _rms(y, g_ref[...])


def _combine(dest3, x, route, g, ys):
    n, d = x.shape
    tm = TM_ROW
    return pl.pallas_call(
        _combine_body,
        grid=(n // tm,),
        in_specs=[pl.BlockSpec((None, 1, TOP_K * tm), lambda i: (i, 0, 0), memory_space=pltpu.SMEM),
                  pl.BlockSpec((tm, d), lambda i: (i, 0)),
                  pl.BlockSpec((tm, LANES), lambda i: (i, 0)),
                  _const_spec(g.shape),
                  pl.BlockSpec(memory_space=pl.ANY)],
        out_specs=pl.BlockSpec((tm, d), lambda i: (i, 0)),
        out_shape=jax.ShapeDtypeStruct((n, d), jnp.float32),
        scratch_shapes=[pltpu.VMEM((TOP_K, tm, d), jnp.float32), pltpu.SemaphoreType.DMA],
        compiler_params=_cparams(("arbitrary",)),
        name="combine",
    )(dest3, x, route, g, ys)


def _row(v):
    return v.reshape(1, -1).astype(jnp.float32)


def kernel(x, positions, ev_norm_mix, ev_w_in, ev_w_pool, ev_pool_scale, ev_ln_g, ev_ln_b, ev_w_spatial, ev_b_spatial, ev_w_out, ev_norm_ffn, ev_w_gate, ev_w_up, ev_w_down, od_norm_attn, od_w_qkv, od_lam_q1, od_lam_k1, od_lam_q2, od_lam_k2, od_subln_g, od_w_o, od_norm_moe, od_w_router, od_we_gate, od_we_up, od_we_down, final_norm):
    assert ev_norm_mix.shape[0] == 1 and od_norm_attn.shape[0] == 1, "one even and one odd layer"
    batch, seq, d = x.shape
    n = batch * seq
    bf16 = jnp.bfloat16
    gm_w = ev_ln_g.shape[-1]

    b_sp_full = jnp.repeat(ev_b_spatial[0].T, gm_w // GMLP_GROUPS, axis=1)
    x1, h1 = _mixer(x.reshape(n, d), seq, _row(ev_norm_mix[0]), ev_w_in[0].astype(bf16), ev_w_pool[0].astype(bf16),
                    _row(ev_pool_scale[0]), _row(ev_ln_g[0]), _row(ev_ln_b[0]), ev_w_spatial[0], b_sp_full,
                    ev_w_out[0].astype(bf16), _row(ev_norm_ffn[0]))
    x2, h2 = _dense_ffn(x1, h1, ev_w_gate.astype(bf16), ev_w_up.astype(bf16), ev_w_down.astype(bf16),
                        _row(od_norm_attn[0]))

    lam_init = 0.8 - 0.6 * math.exp(-0.3 * 1)
    lam = (jnp.exp(jnp.sum(od_lam_q1[0] * od_lam_k1[0])) - jnp.exp(jnp.sum(od_lam_q2[0] * od_lam_k2[0]))
           + lam_init).reshape(1).astype(jnp.float32)
    half = DIFF_HEAD_DIM // 2
    inv_freq = ROPE_THETA ** (-jnp.arange(0, DIFF_HEAD_DIM, 2, dtype=jnp.float32) / DIFF_HEAD_DIM)
    invf = jnp.tile(inv_freq, LANES // half).reshape(1, LANES)
    sign = jnp.tile(jnp.concatenate([-jnp.ones(half), jnp.ones(half)]), LANES // DIFF_HEAD_DIM)
    sign = sign.reshape(1, LANES).astype(jnp.float32)
    w_qkv = od_w_qkv[0].astype(bf16)
    pos = positions.reshape(n, 1)
    q = _proj_rope(h2, w_qkv[:, :d], pos, invf, sign, batch, DIFF_HEAD_DIM ** -0.5)
    k = _proj_rope(h2, w_qkv[:, d:2 * d], pos, invf, sign, batch, 1.0)
    vt = _proj_vt(h2, w_qkv[:, 2 * d:], batch)
    o = _attention(lam, q, k, vt, od_subln_g[0].reshape(-1, 1).astype(jnp.float32), lam_init)

    w_r_pad = jnp.zeros((d, LANES), jnp.float32).at[:, :N_EXPERTS].set(od_w_router[0])
    x3, route, counts = _out_router(o, x2, od_w_o[0].astype(bf16), _row(od_norm_moe[0]), w_r_pad)

    tm_e = TM_MOE
    rows = (n * TOP_K // tm_e + N_EXPERTS) * tm_e
    cnt = counts[0, :N_EXPERTS].astype(jnp.int32)
    padded = (cnt + tm_e - 1) // tm_e * tm_e
    ends = jnp.cumsum(padded)
    offs = ends - padded
    idx = route[:, ROUTE_IDX:ROUTE_IDX + TOP_K].astype(jnp.int32)
    rank = route[:, ROUTE_RANK:ROUTE_RANK + TOP_K].astype(jnp.int32)
    dest = offs[idx] + rank
    tm_r = TM_ROW
    dest3 = dest.reshape(n // tm_r, tm_r, TOP_K).transpose(0, 2, 1).reshape(n // tm_r, 1, TOP_K * tm_r)
    n_valid = (ends[-1] // tm_e).reshape(1).astype(jnp.int32)
    tile_start = jnp.arange(rows // tm_e, dtype=jnp.int32) * tm_e
    tile_expert = jnp.minimum(jnp.sum(tile_start[:, None] >= ends[None, :], axis=1), N_EXPERTS - 1).astype(jnp.int32)

    xs = _dispatch(dest3, x3, rows)
    ys = _moe_ffn(tile_expert, n_valid, xs, _row(od_norm_moe[0]), od_we_gate[0].astype(bf16),
                  od_we_up[0].astype(bf16), od_we_down[0].astype(bf16))
    out = _combine(dest3, x3, route, _row(final_norm), ys)
    return out.reshape(batch, seq, d)
```

```python
import functools
import math

import jax
import jax.numpy as jnp
from jax import lax
from jax.experimental import pallas as pl
from jax.experimental.pallas import tpu as pltpu

EPS = 1e-5
POOL_WINDOWS = (2, 4, 8, 16)
POOL_HALO = 16
GMLP_CHUNK = 128
GMLP_GROUPS = 8
DIFF_HEAD_DIM = 64
HEAD_WIDTH = 2 * DIFF_HEAD_DIM
ROPE_THETA = 10000.0
N_EXPERTS = 8
TOP_K = 2
LANES = 128
NEG_BIG = -1e30
LOG2_E = math.log2(math.e)

VMEM_LIMIT_BYTES = 60 * 1024 * 1024

TM_MIX = 256
TM_FFN = 512
TF_FFN = 512
TM_PROJ = 512
TQ_ATT = 512
TK_ATT = 512
TM_OUT = 256
TM_MOE = 512
TM_ROW = 256


def _cparams(sem):
    return pltpu.CompilerParams(dimension_semantics=sem, vmem_limit_bytes=VMEM_LIMIT_BYTES)


def _const_spec(shape):
    nd = len(shape)
    return pl.BlockSpec(shape, lambda *_: (0,) * nd, pipeline_mode=pl.Buffered(1))


def _rms(x, g):
    return x * lax.rsqrt(jnp.mean(x * x, axis=-1, keepdims=True) + EPS) * g


def _bdot(a, b):
    return jnp.dot(a, b, preferred_element_type=jnp.float32)


def _mixer_body(tiles_per_seq, x_ref, g_ref, w_in_ref, w_pool_ref, pscale_ref, ln_g_ref, ln_b_ref,
                w_sp_ref, b_sp_ref, w_out_ref, g_next_ref, x_out_ref, h_out_ref, ext_ref):
    tm = x_ref.shape[0]
    pool_w = w_pool_ref.shape[0] * w_pool_ref.shape[1]
    pool_g = w_pool_ref.shape[1]
    gm_w = ln_g_ref.shape[1]
    i = pl.program_id(0)
    seq_tile = i % tiles_per_seq

    x = x_ref[...]
    h = _rms(x, g_ref[...]).astype(jnp.bfloat16)
    z = _bdot(h, w_in_ref[...])

    zp = z[:, :pool_w]

    @pl.when(seq_tile == 0)
    def _():
        ext_ref[0:POOL_HALO, :] = jnp.zeros((POOL_HALO, pool_w), jnp.float32)

    ext_ref[POOL_HALO:, :] = zp
    tpos = seq_tile * tm + lax.broadcasted_iota(jnp.int32, (tm, 1), 0)
    pooled_parts = []
    for gi, w in enumerate(POOL_WINDOWS):
        cols = slice(gi * pool_g, (gi + 1) * pool_g)
        p = ext_ref[:, cols]
        k = 1
        while k < w:
            p = p + pltpu.roll(p, k, axis=0)
            k *= 2
        cnt = jnp.minimum(tpos + 1, w).astype(jnp.float32)
        pooled = p[POOL_HALO:, :] / cnt - zp[:, cols]
        pooled_parts.append(_bdot(pooled.astype(jnp.bfloat16), w_pool_ref[gi]))
    ext_ref[0:POOL_HALO, :] = zp[tm - POOL_HALO:, :]
    y_pool = jnp.concatenate(pooled_parts, axis=-1) * pscale_ref[...]

    a = z[:, pool_w:]
    a = 0.5 * a * (1.0 + lax.erf(a * (1.0 / math.sqrt(2.0))))
    u = a[:, :gm_w]
    v = a[:, gm_w:]
    mu = jnp.mean(v, axis=-1, keepdims=True)
    vc = v - mu
    var = jnp.mean(vc * vc, axis=-1, keepdims=True)
    vn = (vc * lax.rsqrt(var + EPS) * ln_g_ref[...] + ln_b_ref[...]).astype(jnp.bfloat16)
    gd = gm_w // GMLP_GROUPS
    row = lax.broadcasted_iota(jnp.int32, (GMLP_CHUNK, GMLP_CHUNK), 0)
    col = lax.broadcasted_iota(jnp.int32, (GMLP_CHUNK, GMLP_CHUNK), 1)
    tril = col <= row
    w_sp = [jnp.where(tril, w_sp_ref[g], 0.0).astype(jnp.bfloat16) for g in range(GMLP_GROUPS)]
    b_sp = b_sp_ref[...]
    rows_out = []
    for c in range(tm // GMLP_CHUNK):
        rs = slice(c * GMLP_CHUNK, (c + 1) * GMLP_CHUNK)
        mixed = jnp.concatenate(
            [_bdot(w_sp[g], vn[rs, g * gd:(g + 1) * gd]) for g in range(GMLP_GROUPS)], axis=-1)
        rows_out.append(u[rs, :] * (mixed + b_sp))
    y_gate = jnp.concatenate(rows_out, axis=0)

    y = jnp.concatenate([y_pool, y_gate], axis=-1).astype(jnp.bfloat16)
    x1 = x + _bdot(y, w_out_ref[...])
    x_out_ref[...] = x1
    h_out_ref[...] = _rms(x1, g_next_ref[...]).astype(jnp.bfloat16)


def _mixer(x, seq, g, w_in, w_pool, pscale, ln_g, ln_b, w_sp, b_sp_full, w_out, g_next):
    n, d = x.shape
    tm = TM_MIX
    assert seq % tm == 0 and tm % GMLP_CHUNK == 0 and tm >= POOL_HALO
    pool_w = w_pool.shape[0] * w_pool.shape[1]
    row_spec = pl.BlockSpec((tm, d), lambda i: (i, 0))
    return pl.pallas_call(
        functools.partial(_mixer_body, seq // tm),
        grid=(n // tm,),
        in_specs=[row_spec, _const_spec(g.shape), _const_spec(w_in.shape), _const_spec(w_pool.shape),
                  _const_spec(pscale.shape), _const_spec(ln_g.shape), _const_spec(ln_b.shape),
                  _const_spec(w_sp.shape), _const_spec(b_sp_full.shape), _const_spec(w_out.shape),
                  _const_spec(g_next.shape)],
        out_specs=[row_spec, row_spec],
        out_shape=[jax.ShapeDtypeStruct((n, d), jnp.float32), jax.ShapeDtypeStruct((n, d), jnp.bfloat16)],
        scratch_shapes=[pltpu.VMEM((tm + POOL_HALO, pool_w), jnp.float32)],
        compiler_params=_cparams(("arbitrary",)),
        name="mixer0",
    )(x, g, w_in, w_pool, pscale, ln_g, ln_b, w_sp, b_sp_full, w_out, g_next)


def _swiglu_step(h, wg_ref, wu_ref, wd_ref, acc_ref):
    a = _bdot(h, wg_ref[...])
    b = _bdot(h, wu_ref[...])
    y = (a * (1.0 / (1.0 + jnp.exp(-a))) * b).astype(jnp.bfloat16)
    acc_ref[...] += _bdot(y, wd_ref[...])


def _dense_ffn_body(te_ref, x_ref, h_ref, wg_ref, wu_ref, wd_ref, g_next_ref, x_out_ref, h_out_ref, acc_ref):
    j = pl.program_id(1)

    @pl.when(j == 0)
    def _():
        acc_ref[...] = jnp.zeros_like(acc_ref)

    _swiglu_step(h_ref[...], wg_ref, wu_ref, wd_ref, acc_ref)

    @pl.when(j == pl.num_programs(1) - 1)
    def _():
        x2 = x_ref[...] + acc_ref[...]
        x_out_ref[...] = x2
        h_out_ref[...] = _rms(x2, g_next_ref[...]).astype(jnp.bfloat16)


def _moe_ffn_body(te_ref, nv_ref, x_ref, g_ref, wg_ref, wu_ref, wd_ref, y_out_ref, h_ref, acc_ref):
    i = pl.program_id(0)
    j = pl.program_id(1)

    @pl.when(i < nv_ref[0])
    def _():
        @pl.when(j == 0)
        def _():
            acc_ref[...] = jnp.zeros_like(acc_ref)
            h_ref[...] = _rms(x_ref[...], g_ref[...]).astype(jnp.bfloat16)

        _swiglu_step(h_ref[...], wg_ref, wu_ref, wd_ref, acc_ref)

        @pl.when(j == pl.num_programs(1) - 1)
        def _():
            y_out_ref[...] = acc_ref[...]

    @pl.when((i >= nv_ref[0]) & (j == 0))
    def _():
        y_out_ref[...] = jnp.zeros_like(y_out_ref)


def _weight_specs(d, ff, tf, expert_of):
    wg_spec = pl.BlockSpec((None, d, tf), lambda i, j, *s: (expert_of(i, *s), 0, j))
    wd_spec = pl.BlockSpec((None, tf, d), lambda i, j, *s: (expert_of(i, *s), j, 0))
    return wg_spec, wg_spec, wd_spec


def _dense_ffn(x, h, wg, wu, wd, g_next):
    n, d = x.shape
    ff = wg.shape[-1]
    tm, tf = TM_FFN, TF_FFN
    assert n % tm == 0 and ff % tf == 0
    row_spec = pl.BlockSpec((tm, d), lambda i, j, *s: (i, 0))
    zero = jnp.zeros((1,), jnp.int32)
    return pl.pallas_call(
        _dense_ffn_body,
        grid_spec=pltpu.PrefetchScalarGridSpec(
            num_scalar_prefetch=1,
            grid=(n // tm, ff // tf),
            in_specs=[row_spec, row_spec, *_weight_specs(d, ff, tf, lambda i, te: te[0]),
                      pl.BlockSpec(g_next.shape, lambda i, j, *s: (0, 0))],
            out_specs=[row_spec, row_spec],
            scratch_shapes=[pltpu.VMEM((tm, d), jnp.float32)],
        ),
        out_shape=[jax.ShapeDtypeStruct((n, d), jnp.float32), jax.ShapeDtypeStruct((n, d), jnp.bfloat16)],
        compiler_params=_cparams(("arbitrary", "arbitrary")),
        name="dense_ffn",
    )(zero, x, h, wg, wu, wd, g_next)


def _moe_ffn(tile_expert, n_valid, xs, g, wg, wu, wd):
    r, d = xs.shape
    ff = wg.shape[-1]
    tm, tf = TM_MOE, TF_FFN
    assert r % tm == 0 and ff % tf == 0
    n_tiles = r // tm

    def row_idx(i, j, te, nv):
        return (jnp.minimum(i, nv[0] - 1), 0)

    def expert_of(i, te, nv):
        return te[jnp.minimum(i, nv[0] - 1)]

    def ff_idx(i, j, nv):
        return jnp.where(i < nv[0], j, ff // tf - 1)

    wg_spec = pl.BlockSpec((None, d, tf), lambda i, j, te, nv: (expert_of(i, te, nv), 0, ff_idx(i, j, nv)))
    wd_spec = pl.BlockSpec((None, tf, d), lambda i, j, te, nv: (expert_of(i, te, nv), ff_idx(i, j, nv), 0))
    row_spec = pl.BlockSpec((tm, d), row_idx)
    return pl.pallas_call(
        _moe_ffn_body,
        grid_spec=pltpu.PrefetchScalarGridSpec(
            num_scalar_prefetch=2,
            grid=(n_tiles, ff // tf),
            in_specs=[row_spec, pl.BlockSpec(g.shape, lambda i, j, *s: (0, 0)), wg_spec, wg_spec, wd_spec],
            out_specs=pl.BlockSpec((tm, d), lambda i, j, *s: (i, 0)),
            scratch_shapes=[pltpu.VMEM((tm, d), jnp.bfloat16), pltpu.VMEM((tm, d), jnp.float32)],
        ),
        out_shape=jax.ShapeDtypeStruct((r, d), jnp.float32),
        compiler_params=_cparams(("arbitrary", "arbitrary")),
        name="moe_ffn",
    )(tile_expert, n_valid, xs, g, wg, wu, wd)


def _proj_rope_body(scale, h_ref, w_ref, pos_ref, invf_ref, sign_ref, o_ref):
    t = _bdot(h_ref[...], w_ref[...])
    half = DIFF_HEAD_DIM // 2
    ang = pos_ref[...].astype(jnp.float32) * invf_ref[...]
    cos = jnp.cos(ang) * scale
    sin = jnp.sin(ang) * sign_ref[...] * scale
    lane = lax.broadcasted_iota(jnp.int32, (1, LANES), 1)
    first_half = (lane % DIFF_HEAD_DIM) < half
    for hd in range(o_ref.shape[0]):
        blk = t[:, hd * HEAD_WIDTH:(hd + 1) * HEAD_WIDTH]
        partner = jnp.where(first_half, pltpu.roll(blk, LANES - half, axis=1), pltpu.roll(blk, half, axis=1))
        o_ref[hd] = (blk * cos + partner * sin).astype(o_ref.dtype)


def _proj_rope(h, w, pos, invf, sign, batch, scale):
    n, d = h.shape
    seq = n // batch
    tm = TM_PROJ
    heads = d // HEAD_WIDTH
    tiles = seq // tm
    return pl.pallas_call(
        functools.partial(_proj_rope_body, scale),
        grid=(n // tm,),
        in_specs=[pl.BlockSpec((tm, d), lambda i: (i, 0)), _const_spec(w.shape),
                  pl.BlockSpec((tm, 1), lambda i: (i, 0)), _const_spec(invf.shape), _const_spec(sign.shape)],
        out_specs=pl.BlockSpec((None, heads, tm, HEAD_WIDTH), lambda i: (i // tiles, 0, i % tiles, 0)),
        out_shape=jax.ShapeDtypeStruct((batch, heads, seq, HEAD_WIDTH), jnp.bfloat16),
        compiler_params=_cparams(("parallel",)),
        name="proj_rope",
    )(h, w, pos, invf, sign)


def _proj_vt_body(h_ref, w_ref, o_ref):
    t = _bdot(h_ref[...], w_ref[...])
    for hd in range(o_ref.shape[0]):
        o_ref[hd, 0] = t[:, hd * HEAD_WIDTH:(hd + 1) * HEAD_WIDTH].T.astype(o_ref.dtype)


def _proj_vt(h, w, batch):
    n, d = h.shape
    seq = n // batch
    tm = TK_ATT
    heads = d // HEAD_WIDTH
    tiles = seq // tm
    return pl.pallas_call(
        _proj_vt_body,
        grid=(n // tm,),
        in_specs=[pl.BlockSpec((tm, d), lambda i: (i, 0)), _const_spec(w.shape)],
        out_specs=pl.BlockSpec((None, heads, 1, HEAD_WIDTH, tm), lambda i: (i // tiles, 0, i % tiles, 0, 0)),
        out_shape=jax.ShapeDtypeStruct((batch, heads, tiles, HEAD_WIDTH, tm), jnp.bfloat16),
        compiler_params=_cparams(("parallel",)),
        name="proj_vt",
    )(h, w)


def _attn_body(lam_init, lam_ref, q_ref, k_ref, vt_ref, g_ref, o_ref, s_ref, cmax_ref, m_ref, l_ref, acc_ref):
    tq = q_ref.shape[0]
    tk = vt_ref.shape[2]
    qi = pl.program_id(2)
    lane = lax.broadcasted_iota(jnp.int32, (1, HEAD_WIDTH), 1)
    q = q_ref[...]
    zero = jnp.zeros_like(q)
    qz = (jnp.where(lane < DIFF_HEAD_DIM, q, zero), jnp.where(lane >= DIFF_HEAD_DIM, q, zero))

    m_ref[...] = jnp.full_like(m_ref, NEG_BIG)
    l_ref[...] = jnp.zeros_like(l_ref)
    acc_ref[...] = jnp.zeros_like(acc_ref)

    def scores(kv, slot, masked, c):
        start = pl.multiple_of(kv * tk, tk)
        kb = k_ref[pl.ds(start, tk), :]
        s = lax.dot_general(kb, qz[c], (((1,), (1,)), ((), ())), preferred_element_type=jnp.float32)
        if masked:
            krow = lax.broadcasted_iota(jnp.int32, (tk, tq), 0)
            qcol = lax.broadcasted_iota(jnp.int32, (tk, tq), 1)
            s = jnp.where(krow <= qcol, s, NEG_BIG)
        s_ref[slot, c] = s
        cmax_ref[slot, c] = jnp.max(s, axis=0, keepdims=True)

    def consume(kv, slot, c):
        m_old = m_ref[c]
        m_new = jnp.maximum(m_old, cmax_ref[slot, c])
        alpha = jnp.exp2(m_old - m_new)
        p = jnp.exp2(s_ref[slot, c] - m_new)
        l_ref[c] = alpha * l_ref[c] + jnp.sum(p, axis=0, keepdims=True)
        acc_ref[c] = alpha * acc_ref[c] + _bdot(vt_ref[kv], p.astype(jnp.bfloat16))
        m_ref[c] = m_new

    for c in range(2):
        scores(qi, 0, True, c)

    def step(j, read_slot):
        kv_cur = jnp.where(j == 0, qi, j - 1)
        for c in range(2):
            scores(j, 1 - read_slot, False, c)
            consume(kv_cur, read_slot, c)

    def body(t, carry):
        step(2 * t, 0)
        step(2 * t + 1, 1)
        return carry

    lax.fori_loop(0, qi // 2, body, 0)
    last = jnp.where(qi == 0, qi, qi - 1)

    @pl.when(qi % 2 == 1)
    def _():
        step(qi - 1, 0)
        for c in range(2):
            consume(last, 1, c)

    @pl.when(qi % 2 == 0)
    def _():
        for c in range(2):
            consume(last, 0, c)

    o = acc_ref[0] / l_ref[0] - lam_ref[0] * (acc_ref[1] / l_ref[1])
    o = o * lax.rsqrt(jnp.mean(o * o, axis=0, keepdims=True) + EPS) * g_ref[...] * (1.0 - lam_init)
    o_ref[...] = o.T.astype(o_ref.dtype)


def _attention(lam, q, k, vt, g_col, lam_init):
    batch, heads, seq, hw = q.shape
    tq, tk = TQ_ATT, TK_ATT
    assert tq == tk and seq % tq == 0
    return pl.pallas_call(
        functools.partial(_attn_body, lam_init),
        grid=(batch, heads, seq // tq),
        in_specs=[pl.BlockSpec(memory_space=pltpu.SMEM),
                  pl.BlockSpec((None, None, tq, hw), lambda b, h, i: (b, h, i, 0)),
                  pl.BlockSpec((None, None, seq, hw), lambda b, h, i: (b, h, 0, 0)),
                  pl.BlockSpec((None, None, seq // tk, hw, tk), lambda b, h, i: (b, h, 0, 0, 0)),
                  pl.BlockSpec(g_col.shape, lambda b, h, i: (0, 0))],
        out_specs=pl.BlockSpec((None, None, tq, hw), lambda b, h, i: (b, h, i, 0)),
        out_shape=jax.ShapeDtypeStruct((batch, heads, seq, hw), jnp.bfloat16),
        scratch_shapes=[pltpu.VMEM((2, 2, tk, tq), jnp.float32), pltpu.VMEM((2, 2, 1, tq), jnp.float32),
                        pltpu.VMEM((2, 1, tq), jnp.float32), pltpu.VMEM((2, 1, tq), jnp.float32),
                        pltpu.VMEM((2, hw, tq), jnp.float32)],
        compiler_params=_cparams(("parallel", "parallel", "arbitrary")),
        name="diff_attn",
    )(lam, q, k, vt, g_col)


ROUTE_IDX, ROUTE_GATE, ROUTE_RANK = 0, 2, 4


def _out_router_body(o_ref, x_ref, w_o_ref, g_ref, w_r_ref, x_out_ref, route_ref, count_ref, run_ref):
    heads, tm, _ = o_ref.shape

    @pl.when((pl.program_id(0) == 0) & (pl.program_id(1) == 0))
    def _():
        run_ref[...] = jnp.zeros_like(run_ref)

    o = jnp.concatenate([o_ref[hd] for hd in range(heads)], axis=-1)
    x3 = x_ref[...] + _bdot(o, w_o_ref[...])
    x_out_ref[...] = x3

    h = _rms(x3, g_ref[...])
    logits = jnp.dot(h, w_r_ref[...], precision=lax.Precision.HIGHEST, preferred_element_type=jnp.float32)
    lane = lax.broadcasted_iota(jnp.int32, (tm, LANES), 1).astype(jnp.float32)
    logits = jnp.where(lane < N_EXPERTS, logits, -jnp.inf)
    m1 = jnp.max(logits, axis=-1, keepdims=True)
    e1 = jnp.min(jnp.where(logits == m1, lane, float(LANES)), axis=-1, keepdims=True)
    sel1 = lane == e1
    rest = jnp.where(sel1, -jnp.inf, logits)
    m2 = jnp.max(rest, axis=-1, keepdims=True)
    e2 = jnp.min(jnp.where(rest == m2, lane, float(LANES)), axis=-1, keepdims=True)
    sel2 = lane == e2
    t = jnp.exp(m2 - m1)
    w1 = 1.0 / (1.0 + t)
    w2 = t / (1.0 + t)

    onehot = (sel1 | sel2).astype(jnp.bfloat16)
    r_i = lax.broadcasted_iota(jnp.int32, (tm, tm), 0)
    c_i = lax.broadcasted_iota(jnp.int32, (tm, tm), 1)
    lower = (c_i < r_i).astype(jnp.bfloat16)
    before = _bdot(lower, onehot) + run_ref[...]
    run_ref[...] += jnp.sum(onehot.astype(jnp.float32), axis=0, keepdims=True)
    rank1 = jnp.sum(jnp.where(sel1, before, 0.0), axis=-1, keepdims=True)
    rank2 = jnp.sum(jnp.where(sel2, before, 0.0), axis=-1, keepdims=True)

    route = jnp.zeros((tm, LANES), jnp.float32)
    for ln, val in ((ROUTE_IDX, e1), (ROUTE_IDX + 1, e2), (ROUTE_GATE, w1), (ROUTE_GATE + 1, w2),
                    (ROUTE_RANK, rank1), (ROUTE_RANK + 1, rank2)):
        route = jnp.where(lane == float(ln), val, route)
    route_ref[...] = route
    count_ref[...] = run_ref[...]


def _out_router(o, x, w_o, g, w_r_pad):
    batch, heads, seq, hw = o.shape
    d = heads * hw
    tm = TM_OUT
    tiles = seq // tm
    n = batch * seq
    return pl.pallas_call(
        _out_router_body,
        grid=(batch, tiles),
        in_specs=[pl.BlockSpec((None, heads, tm, hw), lambda b, i: (b, 0, i, 0)),
                  pl.BlockSpec((tm, d), lambda b, i: (b * tiles + i, 0)),
                  _const_spec(w_o.shape), _const_spec(g.shape), _const_spec(w_r_pad.shape)],
        out_specs=[pl.BlockSpec((tm, d), lambda b, i: (b * tiles + i, 0)),
                   pl.BlockSpec((tm, LANES), lambda b, i: (b * tiles + i, 0)),
                   pl.BlockSpec((1, LANES), lambda b, i: (0, 0))],
        out_shape=[jax.ShapeDtypeStruct((n, d), jnp.float32), jax.ShapeDtypeStruct((n, LANES), jnp.float32),
                   jax.ShapeDtypeStruct((1, LANES), jnp.float32)],
        scratch_shapes=[pltpu.VMEM((1, LANES), jnp.float32)],
        compiler_params=_cparams(("arbitrary", "arbitrary")),
        name="out_router",
    )(o, x, w_o, g, w_r_pad)


def _row_copy(src_ref, dst_ref, sem):
    return pltpu.make_async_copy(src_ref, dst_ref, sem)


def _dispatch_body(dest_ref, x_ref, xs_in_ref, xs_ref, sem):
    del xs_in_ref
    tm = x_ref.shape[0]

    def issue(r, c):
        for k in range(TOP_K):
            _row_copy(x_ref.at[pl.ds(r, 1)], xs_ref.at[pl.ds(dest_ref[0, k * tm + r], 1)], sem).start()
        return c

    lax.fori_loop(0, tm, issue, 0)

    def drain(r, c):
        for k in range(TOP_K):
            _row_copy(x_ref.at[pl.ds(0, 1)], xs_ref.at[pl.ds(0, 1)], sem).wait()
        return c

    lax.fori_loop(0, tm, drain, 0)


def _dispatch(dest3, x, rows):
    n, d = x.shape
    tm = TM_ROW
    zeros = jnp.zeros((rows, d), x.dtype)
    return pl.pallas_call(
        _dispatch_body,
        grid=(n // tm,),
        in_specs=[pl.BlockSpec((None, 1, TOP_K * tm), lambda i: (i, 0, 0), memory_space=pltpu.SMEM),
                  pl.BlockSpec((tm, d), lambda i: (i, 0)),
                  pl.BlockSpec(memory_space=pl.ANY)],
        out_specs=pl.BlockSpec(memory_space=pl.ANY),
        out_shape=jax.ShapeDtypeStruct((rows, d), x.dtype),
        scratch_shapes=[pltpu.SemaphoreType.DMA],
        input_output_aliases={2: 0},
        compiler_params=_cparams(("arbitrary",)),
        name="dispatch",
    )(dest3, x, zeros)


def _combine_body(dest_ref, x_ref, route_ref, g_ref, ys_ref, o_ref, buf_ref, sem):
    tm = x_ref.shape[0]

    def issue(r, c):
        for k in range(TOP_K):
            _row_copy(ys_ref.at[pl.ds(dest_ref[0, k * tm + r], 1)], buf_ref.at[k, pl.ds(r, 1)], sem).start()
        return c

    lax.fori_loop(0, tm, issue, 0)

    def drain(r, c):
        for k in range(TOP_K):
            _row_copy(ys_ref.at[pl.ds(0, 1)], buf_ref.at[k, pl.ds(0, 1)], sem).wait()
        return c

    lax.fori_loop(0, tm, drain, 0)

    route = route_ref[...]
    w1 = route[:, ROUTE_GATE:ROUTE_GATE + 1]
    w2 = route[:, ROUTE_GATE + 1:ROUTE_GATE + 2]
    y = x_ref[...] + w1 * buf_ref[0] + w2 * buf_ref[1]
    o_ref[...] = _rms(y, g_ref[...])


def _combine(dest3, x, route, g, ys):
    n, d = x.shape
    tm = TM_ROW
    return pl.pallas_call(
        _combine_body,
        grid=(n // tm,),
        in_specs=[pl.BlockSpec((None, 1, TOP_K * tm), lambda i: (i, 0, 0), memory_space=pltpu.SMEM),
                  pl.BlockSpec((tm, d), lambda i: (i, 0)),
                  pl.BlockSpec((tm, LANES), lambda i: (i, 0)),
                  _const_spec(g.shape),
                  pl.BlockSpec(memory_space=pl.ANY)],
        out_specs=pl.BlockSpec((tm, d), lambda i: (i, 0)),
        out_shape=jax.ShapeDtypeStruct((n, d), jnp.float32),
        scratch_shapes=[pltpu.VMEM((TOP_K, tm, d), jnp.float32), pltpu.SemaphoreType.DMA],
        compiler_params=_cparams(("arbitrary",)),
        name="combine",
    )(dest3, x, route, g, ys)


def _row(v):
    return v.reshape(1, -1).astype(jnp.float32)


def kernel(x, positions, ev_norm_mix, ev_w_in, ev_w_pool, ev_pool_scale, ev_ln_g, ev_ln_b, ev_w_spatial, ev_b_spatial, ev_w_out, ev_norm_ffn, ev_w_gate, ev_w_up, ev_w_down, od_norm_attn, od_w_qkv, od_lam_q1, od_lam_k1, od_lam_q2, od_lam_k2, od_subln_g, od_w_o, od_norm_moe, od_w_router, od_we_gate, od_we_up, od_we_down, final_norm):
    assert ev_norm_mix.shape[0] == 1 and od_norm_attn.shape[0] == 1, "one even and one odd layer"
    batch, seq, d = x.shape
    n = batch * seq
    bf16 = jnp.bfloat16
    gm_w = ev_ln_g.shape[-1]

    b_sp_full = jnp.repeat(ev_b_spatial[0].T, gm_w // GMLP_GROUPS, axis=1)
    x1, h1 = _mixer(x.reshape(n, d), seq, _row(ev_norm_mix[0]), ev_w_in[0].astype(bf16), ev_w_pool[0].astype(bf16),
                    _row(ev_pool_scale[0]), _row(ev_ln_g[0]), _row(ev_ln_b[0]), ev_w_spatial[0], b_sp_full,
                    ev_w_out[0].astype(bf16), _row(ev_norm_ffn[0]))
    x2, h2 = _dense_ffn(x1, h1, ev_w_gate.astype(bf16), ev_w_up.astype(bf16), ev_w_down.astype(bf16),
                        _row(od_norm_attn[0]))

    lam_init = 0.8 - 0.6 * math.exp(-0.3 * 1)
    lam = (jnp.exp(jnp.sum(od_lam_q1[0] * od_lam_k1[0])) - jnp.exp(jnp.sum(od_lam_q2[0] * od_lam_k2[0]))
           + lam_init).reshape(1).astype(jnp.float32)
    half = DIFF_HEAD_DIM // 2
    inv_freq = ROPE_THETA ** (-jnp.arange(0, DIFF_HEAD_DIM, 2, dtype=jnp.float32) / DIFF_HEAD_DIM)
    invf = jnp.tile(inv_freq, LANES // half).reshape(1, LANES)
    sign = jnp.tile(jnp.concatenate([-jnp.ones(half), jnp.ones(half)]), LANES // DIFF_HEAD_DIM)
    sign = sign.reshape(1, LANES).astype(jnp.float32)
    w_qkv = od_w_qkv[0].astype(bf16)
    pos = positions.reshape(n, 1)
    q = _proj_rope(h2, w_qkv[:, :d], pos, invf, sign, batch, DIFF_HEAD_DIM ** -0.5 * LOG2_E)
    k = _proj_rope(h2, w_qkv[:, d:2 * d], pos, invf, sign, batch, 1.0)
    vt = _proj_vt(h2, w_qkv[:, 2 * d:], batch)
    o = _attention(lam, q, k, vt, od_subln_g[0].reshape(-1, 1).astype(jnp.float32), lam_init)

    w_r_pad = jnp.zeros((d, LANES), jnp.float32).at[:, :N_EXPERTS].set(od_w_router[0])
    x3, route, counts = _out_router(o, x2, od_w_o[0].astype(bf16), _row(od_norm_moe[0]), w_r_pad)

    tm_e = TM_MOE
    rows = (n * TOP_K // tm_e + N_EXPERTS) * tm_e
    cnt = counts[0, :N_EXPERTS].astype(jnp.int32)
    padded = (cnt + tm_e - 1) // tm_e * tm_e
    ends = jnp.cumsum(padded)
    offs = ends - padded
    idx = route[:, ROUTE_IDX:ROUTE_IDX + TOP_K].astype(jnp.int32)
    rank = route[:, ROUTE_RANK:ROUTE_RANK + TOP_K].astype(jnp.int32)
    dest = offs[idx] + rank
    tm_r = TM_ROW
    dest3 = dest.reshape(n // tm_r, tm_r, TOP_K).transpose(0, 2, 1).reshape(n // tm_r, 1, TOP_K * tm_r)
    n_valid = (ends[-1] // tm_e).reshape(1).astype(jnp.int32)
    tile_start = jnp.arange(rows // tm_e, dtype=jnp.int32) * tm_e
    tile_expert = jnp.minimum(jnp.sum(tile_start[:, None] >= ends[None, :], axis=1), N_EXPERTS - 1).astype(jnp.int32)

    xs = _dispatch(dest3, x3, rows)
    ys = _moe_ffn(tile_expert, n_valid, xs, _row(od_norm_moe[0]), od_we_gate[0].astype(bf16),
                  od_we_up[0].astype(bf16), od_we_down[0].astype(bf16))
    out = _combine(dest3, x3, route, _row(final_norm), ys)
    return out.reshape(batch, seq, d)
```

```python
import functools
import math

import jax
import jax.numpy as jnp
from jax import lax
from jax.experimental import pallas as pl
from jax.experimental.pallas import tpu as pltpu

EPS = 1e-5
POOL_WINDOWS = (2, 4, 8, 16)
POOL_HALO = 16
GMLP_CHUNK = 128
GMLP_GROUPS = 8
DIFF_HEAD_DIM = 64
HEAD_WIDTH = 2 * DIFF_HEAD_DIM
ROPE_THETA = 10000.0
N_EXPERTS = 8
TOP_K = 2
LANES = 128
NEG_BIG = -1e30
LOG2_E = math.log2(math.e)

VMEM_LIMIT_BYTES = 60 * 1024 * 1024

TM_MIX = 256
TM_FFN = 512
TF_FFN = 512
TM_PROJ = 512
TQ_ATT = 512
TK_ATT = 512
ATT_UNROLL = 4
TM_OUT = 512
TM_ROUTER = 1024
TM_MOE = 512
TM_ROW = 256
ROW_DMA_UNROLL = 8


def _cparams(sem):
    return pltpu.CompilerParams(dimension_semantics=sem, vmem_limit_bytes=VMEM_LIMIT_BYTES)


def _const_spec(shape):
    nd = len(shape)
    return pl.BlockSpec(shape, lambda *_: (0,) * nd, pipeline_mode=pl.Buffered(1))


def _rms(x, g):
    return x * lax.rsqrt(jnp.mean(x * x, axis=-1, keepdims=True) + EPS) * g


def _bdot(a, b):
    return jnp.dot(a, b, preferred_element_type=jnp.float32)


def _mixer_body(tiles_per_seq, x_ref, g_ref, w_in_ref, w_pool_ref, pscale_ref, ln_g_ref, ln_b_ref,
                w_sp_ref, b_sp_ref, w_out_ref, g_next_ref, x_out_ref, h_out_ref, ext_ref):
    tm = x_ref.shape[0]
    pool_w = w_pool_ref.shape[0] * w_pool_ref.shape[1]
    pool_g = w_pool_ref.shape[1]
    gm_w = ln_g_ref.shape[1]
    i = pl.program_id(0)
    seq_tile = i % tiles_per_seq

    x = x_ref[...]
    h = _rms(x, g_ref[...]).astype(jnp.bfloat16)
    z = _bdot(h, w_in_ref[...])

    zp = z[:, :pool_w]

    @pl.when(seq_tile == 0)
    def _():
        ext_ref[0:POOL_HALO, :] = jnp.zeros((POOL_HALO, pool_w), jnp.float32)

    ext_ref[POOL_HALO:, :] = zp
    tpos = seq_tile * tm + lax.broadcasted_iota(jnp.int32, (tm, 1), 0)
    pooled_parts = []
    for gi, w in enumerate(POOL_WINDOWS):
        cols = slice(gi * pool_g, (gi + 1) * pool_g)
        p = ext_ref[:, cols]
        k = 1
        while k < w:
            p = p + pltpu.roll(p, k, axis=0)
            k *= 2
        cnt = jnp.minimum(tpos + 1, w).astype(jnp.float32)
        pooled = p[POOL_HALO:, :] / cnt - zp[:, cols]
        pooled_parts.append(_bdot(pooled.astype(jnp.bfloat16), w_pool_ref[gi]))
    ext_ref[0:POOL_HALO, :] = zp[tm - POOL_HALO:, :]
    y_pool = jnp.concatenate(pooled_parts, axis=-1) * pscale_ref[...]

    a = z[:, pool_w:]
    a = 0.5 * a * (1.0 + lax.erf(a * (1.0 / math.sqrt(2.0))))
    u = a[:, :gm_w]
    v = a[:, gm_w:]
    mu = jnp.mean(v, axis=-1, keepdims=True)
    vc = v - mu
    var = jnp.mean(vc * vc, axis=-1, keepdims=True)
    vn = (vc * lax.rsqrt(var + EPS) * ln_g_ref[...] + ln_b_ref[...]).astype(jnp.bfloat16)
    gd = gm_w // GMLP_GROUPS
    row = lax.broadcasted_iota(jnp.int32, (GMLP_CHUNK, GMLP_CHUNK), 0)
    col = lax.broadcasted_iota(jnp.int32, (GMLP_CHUNK, GMLP_CHUNK), 1)
    tril = col <= row
    w_sp = [jnp.where(tril, w_sp_ref[g], 0.0).astype(jnp.bfloat16) for g in range(GMLP_GROUPS)]
    b_sp = b_sp_ref[...]
    rows_out = []
    for c in range(tm // GMLP_CHUNK):
        rs = slice(c * GMLP_CHUNK, (c + 1) * GMLP_CHUNK)
        mixed = jnp.concatenate(
            [_bdot(w_sp[g], vn[rs, g * gd:(g + 1) * gd]) for g in range(GMLP_GROUPS)], axis=-1)
        rows_out.append(u[rs, :] * (mixed + b_sp))
    y_gate = jnp.concatenate(rows_out, axis=0)

    y = jnp.concatenate([y_pool, y_gate], axis=-1).astype(jnp.bfloat16)
    x1 = x + _bdot(y, w_out_ref[...])
    x_out_ref[...] = x1
    h_out_ref[...] = _rms(x1, g_next_ref[...]).astype(jnp.bfloat16)


def _mixer(x, seq, g, w_in, w_pool, pscale, ln_g, ln_b, w_sp, b_sp_full, w_out, g_next):
    n, d = x.shape
    tm = TM_MIX
    assert seq % tm == 0 and tm % GMLP_CHUNK == 0 and tm >= POOL_HALO
    pool_w = w_pool.shape[0] * w_pool.shape[1]
    row_spec = pl.BlockSpec((tm, d), lambda i: (i, 0))
    return pl.pallas_call(
        functools.partial(_mixer_body, seq // tm),
        grid=(n // tm,),
        in_specs=[row_spec, _const_spec(g.shape), _const_spec(w_in.shape), _const_spec(w_pool.shape),
                  _const_spec(pscale.shape), _const_spec(ln_g.shape), _const_spec(ln_b.shape),
                  _const_spec(w_sp.shape), _const_spec(b_sp_full.shape), _const_spec(w_out.shape),
                  _const_spec(g_next.shape)],
        out_specs=[row_spec, row_spec],
        out_shape=[jax.ShapeDtypeStruct((n, d), jnp.float32), jax.ShapeDtypeStruct((n, d), jnp.bfloat16)],
        scratch_shapes=[pltpu.VMEM((tm + POOL_HALO, pool_w), jnp.float32)],
        compiler_params=_cparams(("arbitrary",)),
        name="mixer0",
    )(x, g, w_in, w_pool, pscale, ln_g, ln_b, w_sp, b_sp_full, w_out, g_next)


def _swiglu_step(h, wg_ref, wu_ref, wd_ref, acc_ref):
    a = _bdot(h, wg_ref[...])
    b = _bdot(h, wu_ref[...])
    y = (a * (1.0 / (1.0 + jnp.exp(-a))) * b).astype(jnp.bfloat16)
    acc_ref[...] += _bdot(y, wd_ref[...])


def _dense_ffn_body(te_ref, x_ref, h_ref, wg_ref, wu_ref, wd_ref, g_next_ref, x_out_ref, h_out_ref, acc_ref):
    j = pl.program_id(1)

    @pl.when(j == 0)
    def _():
        acc_ref[...] = jnp.zeros_like(acc_ref)

    _swiglu_step(h_ref[...], wg_ref, wu_ref, wd_ref, acc_ref)

    @pl.when(j == pl.num_programs(1) - 1)
    def _():
        x2 = x_ref[...] + acc_ref[...]
        x_out_ref[...] = x2
        h_out_ref[...] = _rms(x2, g_next_ref[...]).astype(jnp.bfloat16)


def _moe_ffn_body(te_ref, nv_ref, x_ref, g_ref, wg_ref, wu_ref, wd_ref, y_out_ref, h_ref, acc_ref):
    i = pl.program_id(0)
    j = pl.program_id(1)

    @pl.when(i < nv_ref[0])
    def _():
        @pl.when(j == 0)
        def _():
            acc_ref[...] = jnp.zeros_like(acc_ref)
            h_ref[...] = _rms(x_ref[...], g_ref[...]).astype(jnp.bfloat16)

        _swiglu_step(h_ref[...], wg_ref, wu_ref, wd_ref, acc_ref)

        @pl.when(j == pl.num_programs(1) - 1)
        def _():
            y_out_ref[...] = acc_ref[...]

    @pl.when((i >= nv_ref[0]) & (j == 0))
    def _():
        y_out_ref[...] = jnp.zeros_like(y_out_ref)


def _weight_specs(d, ff, tf, expert_of):
    wg_spec = pl.BlockSpec((None, d, tf), lambda i, j, *s: (expert_of(i, *s), 0, j))
    wd_spec = pl.BlockSpec((None, tf, d), lambda i, j, *s: (expert_of(i, *s), j, 0))
    return wg_spec, wg_spec, wd_spec


def _dense_ffn(x, h, wg, wu, wd, g_next):
    n, d = x.shape
    ff = wg.shape[-1]
    tm, tf = TM_FFN, TF_FFN
    assert n % tm == 0 and ff % tf == 0
    row_spec = pl.BlockSpec((tm, d), lambda i, j, *s: (i, 0))
    zero = jnp.zeros((1,), jnp.int32)
    return pl.pallas_call(
        _dense_ffn_body,
        grid_spec=pltpu.PrefetchScalarGridSpec(
            num_scalar_prefetch=1,
            grid=(n // tm, ff // tf),
            in_specs=[row_spec, row_spec, *_weight_specs(d, ff, tf, lambda i, te: te[0]),
                      pl.BlockSpec(g_next.shape, lambda i, j, *s: (0, 0))],
            out_specs=[row_spec, row_spec],
            scratch_shapes=[pltpu.VMEM((tm, d), jnp.float32)],
        ),
        out_shape=[jax.ShapeDtypeStruct((n, d), jnp.float32), jax.ShapeDtypeStruct((n, d), jnp.bfloat16)],
        compiler_params=_cparams(("arbitrary", "arbitrary")),
        name="dense_ffn",
    )(zero, x, h, wg, wu, wd, g_next)


def _moe_ffn(tile_expert, n_valid, xs, g, wg, wu, wd):
    r, d = xs.shape
    ff = wg.shape[-1]
    tm, tf = TM_MOE, TF_FFN
    assert r % tm == 0 and ff % tf == 0
    n_tiles = r // tm

    def row_idx(i, j, te, nv):
        return (jnp.minimum(i, nv[0] - 1), 0)

    def expert_of(i, te, nv):
        return te[jnp.minimum(i, nv[0] - 1)]

    def ff_idx(i, j, nv):
        return jnp.where(i < nv[0], j, ff // tf - 1)

    wg_spec = pl.BlockSpec((None, d, tf), lambda i, j, te, nv: (expert_of(i, te, nv), 0, ff_idx(i, j, nv)))
    wd_spec = pl.BlockSpec((None, tf, d), lambda i, j, te, nv: (expert_of(i, te, nv), ff_idx(i, j, nv), 0))
    row_spec = pl.BlockSpec((tm, d), row_idx)
    return pl.pallas_call(
        _moe_ffn_body,
        grid_spec=pltpu.PrefetchScalarGridSpec(
            num_scalar_prefetch=2,
            grid=(n_tiles, ff // tf),
            in_specs=[row_spec, pl.BlockSpec(g.shape, lambda i, j, *s: (0, 0)), wg_spec, wg_spec, wd_spec],
            out_specs=pl.BlockSpec((tm, d), lambda i, j, *s: (i, 0)),
            scratch_shapes=[pltpu.VMEM((tm, d), jnp.bfloat16), pltpu.VMEM((tm, d), jnp.float32)],
        ),
        out_shape=jax.ShapeDtypeStruct((r, d), jnp.float32),
        compiler_params=_cparams(("arbitrary", "arbitrary")),
        name="moe_ffn",
    )(tile_expert, n_valid, xs, g, wg, wu, wd)


def _proj_rope_body(scale, h_ref, w_ref, pos_ref, invf_ref, sign_ref, o_ref):
    t = _bdot(h_ref[...], w_ref[...])
    half = DIFF_HEAD_DIM // 2
    ang = pos_ref[...].astype(jnp.float32) * invf_ref[...]
    cos = jnp.cos(ang) * scale
    sin = jnp.sin(ang) * sign_ref[...] * scale
    lane = lax.broadcasted_iota(jnp.int32, (1, LANES), 1)
    first_half = (lane % DIFF_HEAD_DIM) < half
    for hd in range(o_ref.shape[0]):
        blk = t[:, hd * HEAD_WIDTH:(hd + 1) * HEAD_WIDTH]
        partner = jnp.where(first_half, pltpu.roll(blk, LANES - half, axis=1), pltpu.roll(blk, half, axis=1))
        o_ref[hd] = (blk * cos + partner * sin).astype(o_ref.dtype)


def _proj_rope(h, w, pos, invf, sign, batch, scale):
    n, d = h.shape
    seq = n // batch
    tm = TM_PROJ
    heads = d // HEAD_WIDTH
    tiles = seq // tm
    return pl.pallas_call(
        functools.partial(_proj_rope_body, scale),
        grid=(n // tm,),
        in_specs=[pl.BlockSpec((tm, d), lambda i: (i, 0)), _const_spec(w.shape),
                  pl.BlockSpec((tm, 1), lambda i: (i, 0)), _const_spec(invf.shape), _const_spec(sign.shape)],
        out_specs=pl.BlockSpec((None, heads, tm, HEAD_WIDTH), lambda i: (i // tiles, 0, i % tiles, 0)),
        out_shape=jax.ShapeDtypeStruct((batch, heads, seq, HEAD_WIDTH), jnp.bfloat16),
        compiler_params=_cparams(("parallel",)),
        name="proj_rope",
    )(h, w, pos, invf, sign)


def _proj_vt_body(h_ref, w_ref, o_ref):
    t = _bdot(h_ref[...], w_ref[...])
    for hd in range(o_ref.shape[0]):
        o_ref[hd, 0] = t[:, hd * HEAD_WIDTH:(hd + 1) * HEAD_WIDTH].T.astype(o_ref.dtype)


def _proj_vt(h, w, batch):
    n, d = h.shape
    seq = n // batch
    tm = TK_ATT
    heads = d // HEAD_WIDTH
    tiles = seq // tm
    return pl.pallas_call(
        _proj_vt_body,
        grid=(n // tm,),
        in_specs=[pl.BlockSpec((tm, d), lambda i: (i, 0)), _const_spec(w.shape)],
        out_specs=pl.BlockSpec((None, heads, 1, HEAD_WIDTH, tm), lambda i: (i // tiles, 0, i % tiles, 0, 0)),
        out_shape=jax.ShapeDtypeStruct((batch, heads, tiles, HEAD_WIDTH, tm), jnp.bfloat16),
        compiler_params=_cparams(("parallel",)),
        name="proj_vt",
    )(h, w)


def _attn_body(lam_init, lam_ref, q_ref, k_ref, vt_ref, g_ref, o_ref, s_ref, cmax_ref, m_ref, l_ref, acc_ref):
    tq = q_ref.shape[0]
    tk = vt_ref.shape[2]
    qi = pl.program_id(2)
    lane = lax.broadcasted_iota(jnp.int32, (1, HEAD_WIDTH), 1)
    q = q_ref[...]
    zero = jnp.zeros_like(q)
    qz = (jnp.where(lane < DIFF_HEAD_DIM, q, zero), jnp.where(lane >= DIFF_HEAD_DIM, q, zero))

    m_ref[...] = jnp.full_like(m_ref, NEG_BIG)
    l_ref[...] = jnp.zeros_like(l_ref)
    acc_ref[...] = jnp.zeros_like(acc_ref)

    def scores(kv, slot, masked, c):
        start = pl.multiple_of(kv * tk, tk)
        kb = k_ref[pl.ds(start, tk), :]
        s = lax.dot_general(kb, qz[c], (((1,), (1,)), ((), ())), preferred_element_type=jnp.float32)
        if masked:
            krow = lax.broadcasted_iota(jnp.int32, (tk, tq), 0)
            qcol = lax.broadcasted_iota(jnp.int32, (tk, tq), 1)
            s = jnp.where(krow <= qcol, s, NEG_BIG)
        s_ref[slot, c] = s
        cmax_ref[slot, c] = jnp.max(s, axis=0, keepdims=True)

    def consume(kv, slot, c):
        m_old = m_ref[c]
        m_new = jnp.maximum(m_old, cmax_ref[slot, c])
        alpha = jnp.exp2(m_old - m_new)
        p = jnp.exp2(s_ref[slot, c] - m_new)
        l_ref[c] = alpha * l_ref[c] + jnp.sum(p, axis=0, keepdims=True)
        acc_ref[c] = alpha * acc_ref[c] + _bdot(vt_ref[kv], p.astype(jnp.bfloat16))
        m_ref[c] = m_new

    for c in range(2):
        scores(qi, 0, True, c)

    def step(j, read_slot):
        kv_cur = jnp.where(j == 0, qi, j - 1)
        for c in range(2):
            scores(j, 1 - read_slot, False, c)
            consume(kv_cur, read_slot, c)

    def main(t, carry):
        for u in range(ATT_UNROLL):
            step(ATT_UNROLL * t + u, u % 2)
        return carry

    lax.fori_loop(0, qi // ATT_UNROLL, main, 0)
    done = qi // ATT_UNROLL * ATT_UNROLL

    def pair(t, carry):
        step(done + 2 * t, 0)
        step(done + 2 * t + 1, 1)
        return carry

    lax.fori_loop(0, (qi - done) // 2, pair, 0)
    last = jnp.where(qi == 0, qi, qi - 1)

    @pl.when(qi % 2 == 1)
    def _():
        step(qi - 1, 0)
        for c in range(2):
            consume(last, 1, c)

    @pl.when(qi % 2 == 0)
    def _():
        for c in range(2):
            consume(last, 0, c)

    o = acc_ref[0] / l_ref[0] - lam_ref[0] * (acc_ref[1] / l_ref[1])
    o = o * lax.rsqrt(jnp.mean(o * o, axis=0, keepdims=True) + EPS) * g_ref[...] * (1.0 - lam_init)
    o_ref[...] = o.T.astype(o_ref.dtype)


def _attention(lam, q, k, vt, g_col, lam_init):
    batch, heads, seq, hw = q.shape
    tq, tk = TQ_ATT, TK_ATT
    assert tq == tk and seq % tq == 0
    return pl.pallas_call(
        functools.partial(_attn_body, lam_init),
        grid=(batch, heads, seq // tq),
        in_specs=[pl.BlockSpec(memory_space=pltpu.SMEM),
                  pl.BlockSpec((None, None, tq, hw), lambda b, h, i: (b, h, i, 0)),
                  pl.BlockSpec((None, None, seq, hw), lambda b, h, i: (b, h, 0, 0)),
                  pl.BlockSpec((None, None, seq // tk, hw, tk), lambda b, h, i: (b, h, 0, 0, 0)),
                  pl.BlockSpec(g_col.shape, lambda b, h, i: (0, 0))],
        out_specs=pl.BlockSpec((None, None, tq, hw), lambda b, h, i: (b, h, i, 0)),
        out_shape=jax.ShapeDtypeStruct((batch, heads, seq, hw), jnp.bfloat16),
        scratch_shapes=[pltpu.VMEM((2, 2, tk, tq), jnp.float32), pltpu.VMEM((2, 2, 1, tq), jnp.float32),
                        pltpu.VMEM((2, 1, tq), jnp.float32), pltpu.VMEM((2, 1, tq), jnp.float32),
                        pltpu.VMEM((2, hw, tq), jnp.float32)],
        compiler_params=_cparams(("parallel", "parallel", "arbitrary")),
        name="diff_attn",
    )(lam, q, k, vt, g_col)


ROUTE_IDX, ROUTE_GATE, ROUTE_RANK = 0, 2, 4
ROUTE_ROWS = 8
EXPERT_ROWS = 16


def _out_proj_body(o_ref, x_ref, w_o_ref, x_out_ref):
    o = jnp.concatenate([o_ref[hd] for hd in range(o_ref.shape[0])], axis=-1)
    x_out_ref[...] = x_ref[...] + _bdot(o, w_o_ref[...])


def _out_proj(o, x, w_o):
    batch, heads, seq, hw = o.shape
    d = heads * hw
    tm = TM_OUT
    tiles = seq // tm
    row_spec = pl.BlockSpec((tm, d), lambda b, i: (b * tiles + i, 0))
    return pl.pallas_call(
        _out_proj_body,
        grid=(batch, tiles),
        in_specs=[pl.BlockSpec((None, heads, tm, hw), lambda b, i: (b, 0, i, 0)), row_spec, _const_spec(w_o.shape)],
        out_specs=row_spec,
        out_shape=jax.ShapeDtypeStruct(x.shape, jnp.float32),
        compiler_params=_cparams(("parallel", "parallel")),
        name="out_proj",
    )(o, x, w_o)


def _router_body(x_ref, g_ref, w_rt_ref, route_ref, count_ref, run_ref):
    tm = x_ref.shape[0]

    @pl.when(pl.program_id(0) == 0)
    def _():
        run_ref[...] = jnp.zeros_like(run_ref)

    h = _rms(x_ref[...], g_ref[...])
    h_hi = h.astype(jnp.bfloat16)
    h_lo = (h - h_hi.astype(jnp.float32)).astype(jnp.bfloat16)
    w_rt = w_rt_ref[...]
    w_hi = w_rt.astype(jnp.bfloat16)
    w_lo = (w_rt - w_hi.astype(jnp.float32)).astype(jnp.bfloat16)
    nt = (((1,), (1,)), ((), ()))
    logits = (lax.dot_general(w_hi, h_hi, nt, preferred_element_type=jnp.float32)
              + lax.dot_general(w_lo, h_hi, nt, preferred_element_type=jnp.float32)
              + lax.dot_general(w_hi, h_lo, nt, preferred_element_type=jnp.float32))
    row = lax.broadcasted_iota(jnp.int32, (EXPERT_ROWS, tm), 0).astype(jnp.float32)
    logits = jnp.where(row < N_EXPERTS, logits, -jnp.inf)
    m1 = jnp.max(logits, axis=0, keepdims=True)
    e1 = jnp.min(jnp.where(logits == m1, row, float(EXPERT_ROWS)), axis=0, keepdims=True)
    sel1 = row == e1
    rest = jnp.where(sel1, -jnp.inf, logits)
    m2 = jnp.max(rest, axis=0, keepdims=True)
    e2 = jnp.min(jnp.where(rest == m2, row, float(EXPERT_ROWS)), axis=0, keepdims=True)
    sel2 = row == e2
    t = jnp.exp(m2 - m1)
    w1 = 1.0 / (1.0 + t)
    w2 = t / (1.0 + t)

    onehot = (sel1 | sel2).astype(jnp.bfloat16)
    r_i = lax.broadcasted_iota(jnp.int32, (tm, tm), 0)
    c_i = lax.broadcasted_iota(jnp.int32, (tm, tm), 1)
    earlier = (r_i < c_i).astype(jnp.bfloat16)
    before = _bdot(onehot, earlier) + run_ref[:, 0:1]
    run_ref[...] += jnp.sum(onehot.astype(jnp.float32), axis=1, keepdims=True)
    rank1 = jnp.sum(jnp.where(sel1, before, 0.0), axis=0, keepdims=True)
    rank2 = jnp.sum(jnp.where(sel2, before, 0.0), axis=0, keepdims=True)

    rrow = lax.broadcasted_iota(jnp.int32, (ROUTE_ROWS, tm), 0)
    route = jnp.zeros((ROUTE_ROWS, tm), jnp.float32)
    for rw, val in ((ROUTE_IDX, e1), (ROUTE_IDX + 1, e2), (ROUTE_GATE, w1), (ROUTE_GATE + 1, w2),
                    (ROUTE_RANK, rank1), (ROUTE_RANK + 1, rank2)):
        route = jnp.where(rrow == rw, val, route)
    route_ref[...] = route
    count_ref[...] = run_ref[...]


def _router(x, g, w_rt):
    n, d = x.shape
    tm = TM_ROUTER
    return pl.pallas_call(
        _router_body,
        grid=(n // tm,),
        in_specs=[pl.BlockSpec((tm, d), lambda i: (i, 0)), _const_spec(g.shape), _const_spec(w_rt.shape)],
        out_specs=[pl.BlockSpec((ROUTE_ROWS, tm), lambda i: (0, i)),
                   pl.BlockSpec((EXPERT_ROWS, LANES), lambda i: (0, 0))],
        out_shape=[jax.ShapeDtypeStruct((ROUTE_ROWS, n), jnp.float32),
                   jax.ShapeDtypeStruct((EXPERT_ROWS, LANES), jnp.float32)],
        scratch_shapes=[pltpu.VMEM((EXPERT_ROWS, LANES), jnp.float32)],
        compiler_params=_cparams(("arbitrary",)),
        name="router",
    )(x, g, w_rt)


def _row_copy(src_ref, dst_ref, sem):
    return pltpu.make_async_copy(src_ref, dst_ref, sem)


def _dispatch_body(dest_ref, x_ref, xs_in_ref, xs_ref, sem):
    del xs_in_ref
    tm = x_ref.shape[0]

    def issue(r, c):
        for k in range(TOP_K):
            _row_copy(x_ref.at[pl.ds(r, 1)], xs_ref.at[pl.ds(dest_ref[0, k * tm + r], 1)], sem).start(priority=k)
        return c

    lax.fori_loop(0, tm, issue, 0, unroll=ROW_DMA_UNROLL)

    def drain(r, c):
        for k in range(TOP_K):
            _row_copy(x_ref.at[pl.ds(0, 1)], xs_ref.at[pl.ds(0, 1)], sem).wait()
        return c

    lax.fori_loop(0, tm, drain, 0, unroll=ROW_DMA_UNROLL)


def _dispatch(dest3, x, rows):
    n, d = x.shape
    tm = TM_ROW
    zeros = jnp.zeros((rows, d), x.dtype)
    return pl.pallas_call(
        _dispatch_body,
        grid=(n // tm,),
        in_specs=[pl.BlockSpec((None, 1, TOP_K * tm), lambda i: (i, 0, 0), memory_space=pltpu.SMEM),
                  pl.BlockSpec((tm, d), lambda i: (i, 0)),
                  pl.BlockSpec(memory_space=pl.ANY)],
        out_specs=pl.BlockSpec(memory_space=pl.ANY),
        out_shape=jax.ShapeDtypeStruct((rows, d), x.dtype),
        scratch_shapes=[pltpu.SemaphoreType.DMA],
        input_output_aliases={2: 0},
        compiler_params=_cparams(("arbitrary",)),
        name="dispatch",
    )(dest3, x, zeros)


def _combine_body(dest_ref, x_ref, gate_ref, g_ref, ys_ref, o_ref, buf_ref, sem):
    tm = x_ref.shape[0]

    def issue(r, c):
        for k in range(TOP_K):
            _row_copy(ys_ref.at[pl.ds(dest_ref[0, k * tm + r], 1)], buf_ref.at[k, pl.ds(r, 1)], sem).start(priority=k)
        return c

    lax.fori_loop(0, tm, issue, 0, unroll=ROW_DMA_UNROLL)

    def drain(r, c):
        for k in range(TOP_K):
            _row_copy(ys_ref.at[pl.ds(0, 1)], buf_ref.at[k, pl.ds(0, 1)], sem).wait()
        return c

    lax.fori_loop(0, tm, drain, 0, unroll=ROW_DMA_UNROLL)

    gates = gate_ref[...]
    y = x_ref[...] + gates[:, 0:1] * buf_ref[0] + gates[:, 1:2] * buf_ref[1]
    o_ref[...] = _rms(y, g_ref[...])


def _combine(dest3, x, gates, g, ys):
    n, d = x.shape
    tm = TM_ROW
    return pl.pallas_call(
        _combine_body,
        grid=(n // tm,),
        in_specs=[pl.BlockSpec((None, 1, TOP_K * tm), lambda i: (i, 0, 0), memory_space=pltpu.SMEM),
                  pl.BlockSpec((tm, d), lambda i: (i, 0)),
                  pl.BlockSpec((tm, TOP_K), lambda i: (i, 0)),
                  _const_spec(g.shape),
                  pl.BlockSpec(memory_space=pl.ANY)],
        out_specs=pl.BlockSpec((tm, d), lambda i: (i, 0)),
        out_shape=jax.ShapeDtypeStruct((n, d), jnp.float32),
        scratch_shapes=[pltpu.VMEM((TOP_K, tm, d), jnp.float32), pltpu.SemaphoreType.DMA],
        compiler_params=_cparams(("arbitrary",)),
        name="combine",
    )(dest3, x, gates, g, ys)


def _row(v):
    return v.reshape(1, -1).astype(jnp.float32)


def kernel(x, positions, ev_norm_mix, ev_w_in, ev_w_pool, ev_pool_scale, ev_ln_g, ev_ln_b, ev_w_spatial, ev_b_spatial, ev_w_out, ev_norm_ffn, ev_w_gate, ev_w_up, ev_w_down, od_norm_attn, od_w_qkv, od_lam_q1, od_lam_k1, od_lam_q2, od_lam_k2, od_subln_g, od_w_o, od_norm_moe, od_w_router, od_we_gate, od_we_up, od_we_down, final_norm):
    assert ev_norm_mix.shape[0] == 1 and od_norm_attn.shape[0] == 1, "one even and one odd layer"
    batch, seq, d = x.shape
    n = batch * seq
    bf16 = jnp.bfloat16
    gm_w = ev_ln_g.shape[-1]

    b_sp_full = jnp.repeat(ev_b_spatial[0].T, gm_w // GMLP_GROUPS, axis=1)
    x1, h1 = _mixer(x.reshape(n, d), seq, _row(ev_norm_mix[0]), ev_w_in[0].astype(bf16), ev_w_pool[0].astype(bf16),
                    _row(ev_pool_scale[0]), _row(ev_ln_g[0]), _row(ev_ln_b[0]), ev_w_spatial[0], b_sp_full,
                    ev_w_out[0].astype(bf16), _row(ev_norm_ffn[0]))
    x2, h2 = _dense_ffn(x1, h1, ev_w_gate.astype(bf16), ev_w_up.astype(bf16), ev_w_down.astype(bf16),
                        _row(od_norm_attn[0]))

    lam_init = 0.8 - 0.6 * math.exp(-0.3 * 1)
    lam = (jnp.exp(jnp.sum(od_lam_q1[0] * od_lam_k1[0])) - jnp.exp(jnp.sum(od_lam_q2[0] * od_lam_k2[0]))
           + lam_init).reshape(1).astype(jnp.float32)
    half = DIFF_HEAD_DIM // 2
    inv_freq = ROPE_THETA ** (-jnp.arange(0, DIFF_HEAD_DIM, 2, dtype=jnp.float32) / DIFF_HEAD_DIM)
    invf = jnp.tile(inv_freq, LANES // half).reshape(1, LANES)
    sign = jnp.tile(jnp.concatenate([-jnp.ones(half), jnp.ones(half)]), LANES // DIFF_HEAD_DIM)
    sign = sign.reshape(1, LANES).astype(jnp.float32)
    w_qkv = od_w_qkv[0].astype(bf16)
    pos = positions.reshape(n, 1)
    q = _proj_rope(h2, w_qkv[:, :d], pos, invf, sign, batch, DIFF_HEAD_DIM ** -0.5 * LOG2_E)
    k = _proj_rope(h2, w_qkv[:, d:2 * d], pos, invf, sign, batch, 1.0)
    vt = _proj_vt(h2, w_qkv[:, 2 * d:], batch)
    o = _attention(lam, q, k, vt, od_subln_g[0].reshape(-1, 1).astype(jnp.float32), lam_init)

    w_rt = jnp.zeros((EXPERT_ROWS, d), jnp.float32).at[:N_EXPERTS].set(od_w_router[0].T)
    x3 = _out_proj(o, x2, od_w_o[0].astype(bf16))
    route, counts = _router(x3, _row(od_norm_moe[0]), w_rt)

    tm_e = TM_MOE
    rows = (n * TOP_K // tm_e + N_EXPERTS) * tm_e
    cnt = counts[:N_EXPERTS, 0].astype(jnp.int32)
    padded = (cnt + tm_e - 1) // tm_e * tm_e
    ends = jnp.cumsum(padded)
    offs = ends - padded
    idx = route[ROUTE_IDX:ROUTE_IDX + TOP_K].astype(jnp.int32)
    rank = route[ROUTE_RANK:ROUTE_RANK + TOP_K].astype(jnp.int32)
    dest = offs[idx] + rank
    gates = route[ROUTE_GATE:ROUTE_GATE + TOP_K].T
    tm_r = TM_ROW
    dest3 = dest.reshape(TOP_K, n // tm_r, tm_r).transpose(1, 0, 2).reshape(n // tm_r, 1, TOP_K * tm_r)
    n_valid = (ends[-1] // tm_e).reshape(1).astype(jnp.int32)
    tile_start = jnp.arange(rows // tm_e, dtype=jnp.int32) * tm_e
    tile_expert = jnp.minimum(jnp.sum(tile_start[:, None] >= ends[None, :], axis=1), N_EXPERTS - 1).astype(jnp.int32)

    xs = _dispatch(dest3, x3, rows)
    ys = _moe_ffn(tile_expert, n_valid, xs, _row(od_norm_moe[0]), od_we_gate[0].astype(bf16),
                  od_we_up[0].astype(bf16), od_we_down[0].astype(bf16))
    out = _combine(dest3, x3, gates, _row(final_norm), ys)
    return out.reshape(batch, seq, d)
```

```python
import functools
import math

import jax
import jax.numpy as jnp
from jax import lax
from jax.experimental import pallas as pl
from jax.experimental.pallas import tpu as pltpu

EPS = 1e-5
POOL_WINDOWS = (2, 4, 8, 16)
POOL_HALO = 16
GMLP_CHUNK = 128
GMLP_GROUPS = 8
DIFF_HEAD_DIM = 64
HEAD_WIDTH = 2 * DIFF_HEAD_DIM
ROPE_THETA = 10000.0
N_EXPERTS = 8
TOP_K = 2
LANES = 128
NEG_BIG = -1e30
LOG2_E = math.log2(math.e)

VMEM_LIMIT_BYTES = 60 * 1024 * 1024

TM_MIX = 256
TM_FFN = 512
TF_FFN = 512
TM_PROJ = 512
PROJ_SPLIT = 4
TQ_ATT = 512
TK_ATT = 512
ATT_UNROLL = 4
TM_OUT = 512
TM_ROUTER = 1024
TM_MOE = 512
TM_ROW = 512
ROW_DMA_UNROLL = 8


def _cparams(sem):
    return pltpu.CompilerParams(dimension_semantics=sem, vmem_limit_bytes=VMEM_LIMIT_BYTES)


def _const_spec(shape):
    nd = len(shape)
    return pl.BlockSpec(shape, lambda *_: (0,) * nd, pipeline_mode=pl.Buffered(1))


def _rms(x, g):
    return x * lax.rsqrt(jnp.mean(x * x, axis=-1, keepdims=True) + EPS) * g


def _bdot(a, b):
    return jnp.dot(a, b, preferred_element_type=jnp.float32)


def _mixer_body(tiles_per_seq, x_ref, g_ref, w_in_ref, w_pool_ref, pscale_ref, ln_g_ref, ln_b_ref,
                w_sp_ref, b_sp_ref, w_out_ref, g_next_ref, x_out_ref, h_out_ref, ext_ref):
    tm = x_ref.shape[0]
    pool_w = w_pool_ref.shape[0] * w_pool_ref.shape[1]
    pool_g = w_pool_ref.shape[1]
    gm_w = ln_g_ref.shape[1]
    i = pl.program_id(0)
    seq_tile = i % tiles_per_seq

    x = x_ref[...]
    h = _rms(x, g_ref[...]).astype(jnp.bfloat16)
    z = _bdot(h, w_in_ref[...])

    zp = z[:, :pool_w]

    @pl.when(seq_tile == 0)
    def _():
        ext_ref[0:POOL_HALO, :] = jnp.zeros((POOL_HALO, pool_w), jnp.float32)

    ext_ref[POOL_HALO:, :] = zp
    tpos = seq_tile * tm + lax.broadcasted_iota(jnp.int32, (tm, 1), 0)
    pooled_parts = []
    for gi, w in enumerate(POOL_WINDOWS):
        cols = slice(gi * pool_g, (gi + 1) * pool_g)
        p = ext_ref[:, cols]
        k = 1
        while k < w:
            p = p + pltpu.roll(p, k, axis=0)
            k *= 2
        cnt = jnp.minimum(tpos + 1, w).astype(jnp.float32)
        pooled = p[POOL_HALO:, :] / cnt - zp[:, cols]
        pooled_parts.append(_bdot(pooled.astype(jnp.bfloat16), w_pool_ref[gi]))
    ext_ref[0:POOL_HALO, :] = zp[tm - POOL_HALO:, :]
    y_pool = jnp.concatenate(pooled_parts, axis=-1) * pscale_ref[...]

    a = z[:, pool_w:]
    a = 0.5 * a * (1.0 + lax.erf(a * (1.0 / math.sqrt(2.0))))
    u = a[:, :gm_w]
    v = a[:, gm_w:]
    mu = jnp.mean(v, axis=-1, keepdims=True)
    vc = v - mu
    var = jnp.mean(vc * vc, axis=-1, keepdims=True)
    vn = (vc * lax.rsqrt(var + EPS) * ln_g_ref[...] + ln_b_ref[...]).astype(jnp.bfloat16)
    gd = gm_w // GMLP_GROUPS
    row = lax.broadcasted_iota(jnp.int32, (GMLP_CHUNK, GMLP_CHUNK), 0)
    col = lax.broadcasted_iota(jnp.int32, (GMLP_CHUNK, GMLP_CHUNK), 1)
    tril = col <= row
    w_sp = [jnp.where(tril, w_sp_ref[g], 0.0).astype(jnp.bfloat16) for g in range(GMLP_GROUPS)]
    b_sp = b_sp_ref[...]
    rows_out = []
    for c in range(tm // GMLP_CHUNK):
        rs = slice(c * GMLP_CHUNK, (c + 1) * GMLP_CHUNK)
        mixed = jnp.concatenate(
            [_bdot(w_sp[g], vn[rs, g * gd:(g + 1) * gd]) for g in range(GMLP_GROUPS)], axis=-1)
        rows_out.append(u[rs, :] * (mixed + b_sp))
    y_gate = jnp.concatenate(rows_out, axis=0)

    y = jnp.concatenate([y_pool, y_gate], axis=-1).astype(jnp.bfloat16)
    x1 = x + _bdot(y, w_out_ref[...])
    x_out_ref[...] = x1
    h_out_ref[...] = _rms(x1, g_next_ref[...]).astype(jnp.bfloat16)


def _mixer(x, seq, g, w_in, w_pool, pscale, ln_g, ln_b, w_sp, b_sp_full, w_out, g_next):
    n, d = x.shape
    tm = TM_MIX
    assert seq % tm == 0 and tm % GMLP_CHUNK == 0 and tm >= POOL_HALO
    pool_w = w_pool.shape[0] * w_pool.shape[1]
    row_spec = pl.BlockSpec((tm, d), lambda i: (i, 0))
    return pl.pallas_call(
        functools.partial(_mixer_body, seq // tm),
        grid=(n // tm,),
        in_specs=[row_spec, _const_spec(g.shape), _const_spec(w_in.shape), _const_spec(w_pool.shape),
                  _const_spec(pscale.shape), _const_spec(ln_g.shape), _const_spec(ln_b.shape),
                  _const_spec(w_sp.shape), _const_spec(b_sp_full.shape), _const_spec(w_out.shape),
                  _const_spec(g_next.shape)],
        out_specs=[row_spec, row_spec],
        out_shape=[jax.ShapeDtypeStruct((n, d), jnp.float32), jax.ShapeDtypeStruct((n, d), jnp.bfloat16)],
        scratch_shapes=[pltpu.VMEM((tm + POOL_HALO, pool_w), jnp.float32)],
        compiler_params=_cparams(("arbitrary",)),
        name="mixer0",
    )(x, g, w_in, w_pool, pscale, ln_g, ln_b, w_sp, b_sp_full, w_out, g_next)


def _swiglu_step(h, wg_ref, wu_ref, wd_ref):
    a = _bdot(h, wg_ref[...])
    b = _bdot(h, wu_ref[...])
    y = (a * (1.0 / (1.0 + jnp.exp(-a))) * b).astype(jnp.bfloat16)
    return _bdot(y, wd_ref[...])


def _dense_ffn_body(te_ref, x_ref, h_ref, wg_ref, wu_ref, wd_ref, g_next_ref, x_out_ref, h_out_ref, acc_ref):
    j = pl.program_id(1)

    @pl.when(j == 0)
    def _():
        acc_ref[...] = _swiglu_step(h_ref[...], wg_ref, wu_ref, wd_ref)

    @pl.when(j > 0)
    def _():
        acc_ref[...] += _swiglu_step(h_ref[...], wg_ref, wu_ref, wd_ref)

    @pl.when(j == pl.num_programs(1) - 1)
    def _():
        x2 = x_ref[...] + acc_ref[...]
        x_out_ref[...] = x2
        h_out_ref[...] = _rms(x2, g_next_ref[...]).astype(jnp.bfloat16)


def _moe_ffn_body(te_ref, nv_ref, x_ref, g_ref, wg_ref, wu_ref, wd_ref, y_out_ref, h_ref, acc_ref):
    i = pl.program_id(0)
    j = pl.program_id(1)

    @pl.when(i < nv_ref[0])
    def _():
        @pl.when(j == 0)
        def _():
            h = _rms(x_ref[...], g_ref[...]).astype(jnp.bfloat16)
            h_ref[...] = h
            acc_ref[...] = _swiglu_step(h, wg_ref, wu_ref, wd_ref)

        @pl.when(j > 0)
        def _():
            acc_ref[...] += _swiglu_step(h_ref[...], wg_ref, wu_ref, wd_ref)

        @pl.when(j == pl.num_programs(1) - 1)
        def _():
            y_out_ref[...] = acc_ref[...]

    @pl.when((i >= nv_ref[0]) & (j == 0))
    def _():
        y_out_ref[...] = jnp.zeros_like(y_out_ref)


def _weight_specs(d, ff, tf, expert_of):
    wg_spec = pl.BlockSpec((None, d, tf), lambda i, j, *s: (expert_of(i, *s), 0, j))
    wd_spec = pl.BlockSpec((None, tf, d), lambda i, j, *s: (expert_of(i, *s), j, 0))
    return wg_spec, wg_spec, wd_spec


def _dense_ffn(x, h, wg, wu, wd, g_next):
    n, d = x.shape
    ff = wg.shape[-1]
    tm, tf = TM_FFN, TF_FFN
    assert n % tm == 0 and ff % tf == 0
    row_spec = pl.BlockSpec((tm, d), lambda i, j, *s: (i, 0))
    zero = jnp.zeros((1,), jnp.int32)
    return pl.pallas_call(
        _dense_ffn_body,
        grid_spec=pltpu.PrefetchScalarGridSpec(
            num_scalar_prefetch=1,
            grid=(n // tm, ff // tf),
            in_specs=[row_spec, row_spec, *_weight_specs(d, ff, tf, lambda i, te: te[0]),
                      pl.BlockSpec(g_next.shape, lambda i, j, *s: (0, 0))],
            out_specs=[row_spec, row_spec],
            scratch_shapes=[pltpu.VMEM((tm, d), jnp.float32)],
        ),
        out_shape=[jax.ShapeDtypeStruct((n, d), jnp.float32), jax.ShapeDtypeStruct((n, d), jnp.bfloat16)],
        compiler_params=_cparams(("arbitrary", "arbitrary")),
        name="dense_ffn",
    )(zero, x, h, wg, wu, wd, g_next)


def _moe_ffn(tile_expert, n_valid, xs, g, wg, wu, wd):
    r, d = xs.shape
    ff = wg.shape[-1]
    tm, tf = TM_MOE, TF_FFN
    assert r % tm == 0 and ff % tf == 0
    n_tiles = r // tm

    def row_idx(i, j, te, nv):
        return (jnp.minimum(i, nv[0] - 1), 0)

    def expert_of(i, te, nv):
        return te[jnp.minimum(i, nv[0] - 1)]

    def ff_idx(i, j, nv):
        return jnp.where(i < nv[0], j, ff // tf - 1)

    wg_spec = pl.BlockSpec((None, d, tf), lambda i, j, te, nv: (expert_of(i, te, nv), 0, ff_idx(i, j, nv)))
    wd_spec = pl.BlockSpec((None, tf, d), lambda i, j, te, nv: (expert_of(i, te, nv), ff_idx(i, j, nv), 0))
    row_spec = pl.BlockSpec((tm, d), row_idx)
    return pl.pallas_call(
        _moe_ffn_body,
        grid_spec=pltpu.PrefetchScalarGridSpec(
            num_scalar_prefetch=2,
            grid=(n_tiles, ff // tf),
            in_specs=[row_spec, pl.BlockSpec(g.shape, lambda i, j, *s: (0, 0)), wg_spec, wg_spec, wd_spec],
            out_specs=pl.BlockSpec((tm, d), lambda i, j, *s: (i, 0)),
            scratch_shapes=[pltpu.VMEM((tm, d), jnp.bfloat16), pltpu.VMEM((tm, d), jnp.float32)],
        ),
        out_shape=jax.ShapeDtypeStruct((r, d), jnp.float32),
        compiler_params=_cparams(("arbitrary", "arbitrary")),
        name="moe_ffn",
    )(tile_expert, n_valid, xs, g, wg, wu, wd)


def _proj_rope_body(scale, h_ref, w_ref, pos_ref, invf_ref, sign_ref, o_ref):
    half = DIFF_HEAD_DIM // 2
    ang = pos_ref[...].astype(jnp.float32) * invf_ref[...]
    cos = jnp.cos(ang) * scale
    sin = jnp.sin(ang) * sign_ref[...] * scale
    lane = lax.broadcasted_iota(jnp.int32, (1, LANES), 1)
    first_half = (lane % DIFF_HEAD_DIM) < half
    rows = h_ref.shape[0] // PROJ_SPLIT
    for part in range(PROJ_SPLIT):
        rs = slice(part * rows, (part + 1) * rows)
        t = _bdot(h_ref[rs, :], w_ref[...])
        for hd in range(o_ref.shape[0]):
            blk = t[:, hd * HEAD_WIDTH:(hd + 1) * HEAD_WIDTH]
            partner = jnp.where(first_half, pltpu.roll(blk, LANES - half, axis=1), pltpu.roll(blk, half, axis=1))
            o_ref[hd, rs, :] = (blk * cos[rs] + partner * sin[rs]).astype(o_ref.dtype)


def _proj_rope(h, w, pos, invf, sign, batch, scale):
    n, d = h.shape
    seq = n // batch
    tm = TM_PROJ
    heads = d // HEAD_WIDTH
    tiles = seq // tm
    return pl.pallas_call(
        functools.partial(_proj_rope_body, scale),
        grid=(n // tm,),
        in_specs=[pl.BlockSpec((tm, d), lambda i: (i, 0)), _const_spec(w.shape),
                  pl.BlockSpec((tm, 1), lambda i: (i, 0)), _const_spec(invf.shape), _const_spec(sign.shape)],
        out_specs=pl.BlockSpec((None, heads, tm, HEAD_WIDTH), lambda i: (i // tiles, 0, i % tiles, 0)),
        out_shape=jax.ShapeDtypeStruct((batch, heads, seq, HEAD_WIDTH), jnp.bfloat16),
        compiler_params=_cparams(("parallel",)),
        name="proj_rope",
    )(h, w, pos, invf, sign)


def _proj_vt_body(h_ref, w_ref, o_ref):
    t = _bdot(h_ref[...], w_ref[...])
    for hd in range(o_ref.shape[0]):
        o_ref[hd, 0] = t[:, hd * HEAD_WIDTH:(hd + 1) * HEAD_WIDTH].T.astype(o_ref.dtype)


def _proj_vt(h, w, batch):
    n, d = h.shape
    seq = n // batch
    tm = TK_ATT
    heads = d // HEAD_WIDTH
    tiles = seq // tm
    return pl.pallas_call(
        _proj_vt_body,
        grid=(n // tm,),
        in_specs=[pl.BlockSpec((tm, d), lambda i: (i, 0)), _const_spec(w.shape)],
        out_specs=pl.BlockSpec((None, heads, 1, HEAD_WIDTH, tm), lambda i: (i // tiles, 0, i % tiles, 0, 0)),
        out_shape=jax.ShapeDtypeStruct((batch, heads, tiles, HEAD_WIDTH, tm), jnp.bfloat16),
        compiler_params=_cparams(("parallel",)),
        name="proj_vt",
    )(h, w)


def _attn_body(lam_init, lam_ref, q_ref, k_ref, vt_ref, g_ref, o_ref, s_ref, cmax_ref, m_ref, l_ref, acc_ref, bias_ref):
    tq = q_ref.shape[0]
    tk = vt_ref.shape[2]
    qi = pl.program_id(2)
    lane = lax.broadcasted_iota(jnp.int32, (1, HEAD_WIDTH), 1)
    q = q_ref[...]
    zero = jnp.zeros_like(q)
    qz = (jnp.where(lane < DIFF_HEAD_DIM, q, zero), jnp.where(lane >= DIFF_HEAD_DIM, q, zero))

    m_ref[...] = jnp.full_like(m_ref, NEG_BIG)
    l_ref[...] = jnp.zeros_like(l_ref)
    acc_ref[...] = jnp.zeros_like(acc_ref)

    @pl.when(qi == 0)
    def _():
        krow = lax.broadcasted_iota(jnp.int32, (tk, tq), 0)
        qcol = lax.broadcasted_iota(jnp.int32, (tk, tq), 1)
        bias_ref[...] = jnp.where(krow <= qcol, 0.0, NEG_BIG)

    def scores(kv, slot, masked, c):
        start = pl.multiple_of(kv * tk, tk)
        kb = k_ref[pl.ds(start, tk), :]
        s = lax.dot_general(kb, qz[c], (((1,), (1,)), ((), ())), preferred_element_type=jnp.float32)
        if masked:
            s = s + bias_ref[...]
        s_ref[slot, c] = s
        cmax_ref[slot, c] = jnp.max(s, axis=0, keepdims=True)

    def consume(kv, slot, c):
        m_old = m_ref[c]
        m_new = jnp.maximum(m_old, cmax_ref[slot, c])
        alpha = jnp.exp2(m_old - m_new)
        p = jnp.exp2(s_ref[slot, c] - m_new)
        l_ref[c] = alpha * l_ref[c] + jnp.sum(p, axis=0, keepdims=True)
        acc_ref[c] = alpha * acc_ref[c] + _bdot(vt_ref[kv], p.astype(jnp.bfloat16))
        m_ref[c] = m_new

    for c in range(2):
        scores(qi, 0, True, c)

    def step(j, read_slot):
        kv_cur = jnp.where(j == 0, qi, j - 1)
        for c in range(2):
            scores(j, 1 - read_slot, False, c)
            consume(kv_cur, read_slot, c)

    def main(t, carry):
        for u in range(ATT_UNROLL):
            step(ATT_UNROLL * t + u, u % 2)
        return carry

    lax.fori_loop(0, qi // ATT_UNROLL, main, 0)
    done = qi // ATT_UNROLL * ATT_UNROLL

    def pair(t, carry):
        step(done + 2 * t, 0)
        step(done + 2 * t + 1, 1)
        return carry

    lax.fori_loop(0, (qi - done) // 2, pair, 0)
    last = jnp.where(qi == 0, qi, qi - 1)

    @pl.when(qi % 2 == 1)
    def _():
        step(qi - 1, 0)
        for c in range(2):
            consume(last, 1, c)

    @pl.when(qi % 2 == 0)
    def _():
        for c in range(2):
            consume(last, 0, c)

    o = acc_ref[0] / l_ref[0] - lam_ref[0] * (acc_ref[1] / l_ref[1])
    o = o * lax.rsqrt(jnp.mean(o * o, axis=0, keepdims=True) + EPS) * g_ref[...] * (1.0 - lam_init)
    o_ref[...] = o.T.astype(o_ref.dtype)


def _attention(lam, q, k, vt, g_col, lam_init):
    batch, heads, seq, hw = q.shape
    tq, tk = TQ_ATT, TK_ATT
    assert tq == tk and seq % tq == 0
    return pl.pallas_call(
        functools.partial(_attn_body, lam_init),
        grid=(batch, heads, seq // tq),
        in_specs=[pl.BlockSpec(memory_space=pltpu.SMEM),
                  pl.BlockSpec((None, None, tq, hw), lambda b, h, i: (b, h, i, 0)),
                  pl.BlockSpec((None, None, seq, hw), lambda b, h, i: (b, h, 0, 0)),
                  pl.BlockSpec((None, None, seq // tk, hw, tk), lambda b, h, i: (b, h, 0, 0, 0)),
                  pl.BlockSpec(g_col.shape, lambda b, h, i: (0, 0))],
        out_specs=pl.BlockSpec((None, None, tq, hw), lambda b, h, i: (b, h, i, 0)),
        out_shape=jax.ShapeDtypeStruct((batch, heads, seq, hw), jnp.bfloat16),
        scratch_shapes=[pltpu.VMEM((2, 2, tk, tq), jnp.float32), pltpu.VMEM((2, 2, 1, tq), jnp.float32),
                        pltpu.VMEM((2, 1, tq), jnp.float32), pltpu.VMEM((2, 1, tq), jnp.float32),
                        pltpu.VMEM((2, hw, tq), jnp.float32), pltpu.VMEM((tk, tq), jnp.float32)],
        compiler_params=_cparams(("parallel", "parallel", "arbitrary")),
        name="diff_attn",
    )(lam, q, k, vt, g_col)


ROUTE_IDX, ROUTE_GATE, ROUTE_RANK = 0, 2, 4
ROUTE_ROWS = 8
EXPERT_ROWS = 16


def _out_proj_body(o_ref, x_ref, w_o_ref, x_out_ref):
    o = jnp.concatenate([o_ref[hd] for hd in range(o_ref.shape[0])], axis=-1)
    x_out_ref[...] = x_ref[...] + _bdot(o, w_o_ref[...])


def _out_proj(o, x, w_o):
    batch, heads, seq, hw = o.shape
    d = heads * hw
    tm = TM_OUT
    tiles = seq // tm
    row_spec = pl.BlockSpec((tm, d), lambda b, i: (b * tiles + i, 0))
    return pl.pallas_call(
        _out_proj_body,
        grid=(batch, tiles),
        in_specs=[pl.BlockSpec((None, heads, tm, hw), lambda b, i: (b, 0, i, 0)), row_spec, _const_spec(w_o.shape)],
        out_specs=row_spec,
        out_shape=jax.ShapeDtypeStruct(x.shape, jnp.float32),
        compiler_params=_cparams(("parallel", "parallel")),
        name="out_proj",
    )(o, x, w_o)


def _router_body(x_ref, g_ref, w_rt_ref, route_ref, count_ref, run_ref):
    tm = x_ref.shape[0]

    @pl.when(pl.program_id(0) == 0)
    def _():
        run_ref[...] = jnp.zeros_like(run_ref)

    h = _rms(x_ref[...], g_ref[...])
    h_hi = h.astype(jnp.bfloat16)
    h_lo = (h - h_hi.astype(jnp.float32)).astype(jnp.bfloat16)
    w_rt = w_rt_ref[...]
    w_hi = w_rt.astype(jnp.bfloat16)
    w_lo = (w_rt - w_hi.astype(jnp.float32)).astype(jnp.bfloat16)
    nt = (((1,), (1,)), ((), ()))
    logits = (lax.dot_general(w_hi, h_hi, nt, preferred_element_type=jnp.float32)
              + lax.dot_general(w_lo, h_hi, nt, preferred_element_type=jnp.float32)
              + lax.dot_general(w_hi, h_lo, nt, preferred_element_type=jnp.float32))
    row = lax.broadcasted_iota(jnp.int32, (EXPERT_ROWS, tm), 0).astype(jnp.float32)
    logits = jnp.where(row < N_EXPERTS, logits, -jnp.inf)
    m1 = jnp.max(logits, axis=0, keepdims=True)
    e1 = jnp.min(jnp.where(logits == m1, row, float(EXPERT_ROWS)), axis=0, keepdims=True)
    sel1 = row == e1
    rest = jnp.where(sel1, -jnp.inf, logits)
    m2 = jnp.max(rest, axis=0, keepdims=True)
    e2 = jnp.min(jnp.where(rest == m2, row, float(EXPERT_ROWS)), axis=0, keepdims=True)
    sel2 = row == e2
    t = jnp.exp(m2 - m1)
    w1 = 1.0 / (1.0 + t)
    w2 = t / (1.0 + t)

    onehot = (sel1 | sel2).astype(jnp.bfloat16)
    r_i = lax.broadcasted_iota(jnp.int32, (tm, tm), 0)
    c_i = lax.broadcasted_iota(jnp.int32, (tm, tm), 1)
    earlier = (r_i < c_i).astype(jnp.bfloat16)
    before = _bdot(onehot, earlier) + run_ref[:, 0:1]
    run_ref[...] += jnp.sum(onehot.astype(jnp.float32), axis=1, keepdims=True)
    rank1 = jnp.sum(jnp.where(sel1, before, 0.0), axis=0, keepdims=True)
    rank2 = jnp.sum(jnp.where(sel2, before, 0.0), axis=0, keepdims=True)

    rrow = lax.broadcasted_iota(jnp.int32, (ROUTE_ROWS, tm), 0)
    route = jnp.zeros((ROUTE_ROWS, tm), jnp.float32)
    for rw, val in ((ROUTE_IDX, e1), (ROUTE_IDX + 1, e2), (ROUTE_GATE, w1), (ROUTE_GATE + 1, w2),
                    (ROUTE_RANK, rank1), (ROUTE_RANK + 1, rank2)):
        route = jnp.where(rrow == rw, val, route)
    route_ref[...] = route
    count_ref[...] = run_ref[...]


def _router(x, g, w_rt):
    n, d = x.shape
    tm = TM_ROUTER
    return pl.pallas_call(
        _router_body,
        grid=(n // tm,),
        in_specs=[pl.BlockSpec((tm, d), lambda i: (i, 0)), _const_spec(g.shape), _const_spec(w_rt.shape)],
        out_specs=[pl.BlockSpec((ROUTE_ROWS, tm), lambda i: (0, i)),
                   pl.BlockSpec((EXPERT_ROWS, LANES), lambda i: (0, 0))],
        out_shape=[jax.ShapeDtypeStruct((ROUTE_ROWS, n), jnp.float32),
                   jax.ShapeDtypeStruct((EXPERT_ROWS, LANES), jnp.float32)],
        scratch_shapes=[pltpu.VMEM((EXPERT_ROWS, LANES), jnp.float32)],
        compiler_params=_cparams(("arbitrary",)),
        name="router",
    )(x, g, w_rt)


def _row_copy(src_ref, dst_ref, sem):
    return pltpu.make_async_copy(src_ref, dst_ref, sem)


def _dispatch_body(dest_ref, x_ref, xs_in_ref, xs_ref, sem):
    del xs_in_ref
    tm = x_ref.shape[0]

    def issue(r, c):
        for k in range(TOP_K):
            _row_copy(x_ref.at[pl.ds(r, 1)], xs_ref.at[pl.ds(dest_ref[0, k * tm + r], 1)], sem).start(priority=k)
        return c

    lax.fori_loop(0, tm, issue, 0, unroll=ROW_DMA_UNROLL)

    def drain(r, c):
        for k in range(TOP_K):
            _row_copy(x_ref.at[pl.ds(0, 1)], xs_ref.at[pl.ds(0, 1)], sem).wait()
        return c

    lax.fori_loop(0, tm, drain, 0, unroll=ROW_DMA_UNROLL)


def _dispatch(dest3, x, rows):
    n, d = x.shape
    tm = TM_ROW
    zeros = jnp.zeros((rows, d), x.dtype)
    return pl.pallas_call(
        _dispatch_body,
        grid=(n // tm,),
        in_specs=[pl.BlockSpec((None, 1, TOP_K * tm), lambda i: (i, 0, 0), memory_space=pltpu.SMEM),
                  pl.BlockSpec((tm, d), lambda i: (i, 0)),
                  pl.BlockSpec(memory_space=pl.ANY)],
        out_specs=pl.BlockSpec(memory_space=pl.ANY),
        out_shape=jax.ShapeDtypeStruct((rows, d), x.dtype),
        scratch_shapes=[pltpu.SemaphoreType.DMA],
        input_output_aliases={2: 0},
        compiler_params=_cparams(("arbitrary",)),
        name="dispatch",
    )(dest3, x, zeros)


def _combine_body(dest_ref, x_ref, gate_ref, g_ref, ys_ref, o_ref, buf_ref, sem):
    tm = x_ref.shape[0]

    def issue(r, c):
        for k in range(TOP_K):
            _row_copy(ys_ref.at[pl.ds(dest_ref[0, k * tm + r], 1)], buf_ref.at[k, pl.ds(r, 1)], sem).start(priority=k)
        return c

    lax.fori_loop(0, tm, issue, 0, unroll=ROW_DMA_UNROLL)

    def drain(r, c):
        for k in range(TOP_K):
            _row_copy(ys_ref.at[pl.ds(0, 1)], buf_ref.at[k, pl.ds(0, 1)], sem).wait()
        return c

    lax.fori_loop(0, tm, drain, 0, unroll=ROW_DMA_UNROLL)

    gates = gate_ref[...]
    y = x_ref[...] + gates[:, 0:1] * buf_ref[0] + gates[:, 1:2] * buf_ref[1]
    o_ref[...] = _rms(y, g_ref[...])


def _combine(dest3, x, gates, g, ys):
    n, d = x.shape
    tm = TM_ROW
    return pl.pallas_call(
        _combine_body,
        grid=(n // tm,),
        in_specs=[pl.BlockSpec((None, 1, TOP_K * tm), lambda i: (i, 0, 0), memory_space=pltpu.SMEM),
                  pl.BlockSpec((tm, d), lambda i: (i, 0)),
                  pl.BlockSpec((tm, TOP_K), lambda i: (i, 0)),
                  _const_spec(g.shape),
                  pl.BlockSpec(memory_space=pl.ANY)],
        out_specs=pl.BlockSpec((tm, d), lambda i: (i, 0)),
        out_shape=jax.ShapeDtypeStruct((n, d), jnp.float32),
        scratch_shapes=[pltpu.VMEM((TOP_K, tm, d), jnp.float32), pltpu.SemaphoreType.DMA],
        compiler_params=_cparams(("arbitrary",)),
        name="combine",
    )(dest3, x, gates, g, ys)


def _row(v):
    return v.reshape(1, -1).astype(jnp.float32)


def kernel(x, positions, ev_norm_mix, ev_w_in, ev_w_pool, ev_pool_scale, ev_ln_g, ev_ln_b, ev_w_spatial, ev_b_spatial, ev_w_out, ev_norm_ffn, ev_w_gate, ev_w_up, ev_w_down, od_norm_attn, od_w_qkv, od_lam_q1, od_lam_k1, od_lam_q2, od_lam_k2, od_subln_g, od_w_o, od_norm_moe, od_w_router, od_we_gate, od_we_up, od_we_down, final_norm):
    assert ev_norm_mix.shape[0] == 1 and od_norm_attn.shape[0] == 1, "one even and one odd layer"
    batch, seq, d = x.shape
    n = batch * seq
    bf16 = jnp.bfloat16
    gm_w = ev_ln_g.shape[-1]

    b_sp_full = jnp.repeat(ev_b_spatial[0].T, gm_w // GMLP_GROUPS, axis=1)
    x1, h1 = _mixer(x.reshape(n, d), seq, _row(ev_norm_mix[0]), ev_w_in[0].astype(bf16), ev_w_pool[0].astype(bf16),
                    _row(ev_pool_scale[0]), _row(ev_ln_g[0]), _row(ev_ln_b[0]), ev_w_spatial[0], b_sp_full,
                    ev_w_out[0].astype(bf16), _row(ev_norm_ffn[0]))
    x2, h2 = _dense_ffn(x1, h1, ev_w_gate.astype(bf16), ev_w_up.astype(bf16), ev_w_down.astype(bf16),
                        _row(od_norm_attn[0]))

    lam_init = 0.8 - 0.6 * math.exp(-0.3 * 1)
    lam = (jnp.exp(jnp.sum(od_lam_q1[0] * od_lam_k1[0])) - jnp.exp(jnp.sum(od_lam_q2[0] * od_lam_k2[0]))
           + lam_init).reshape(1).astype(jnp.float32)
    half = DIFF_HEAD_DIM // 2
    inv_freq = ROPE_THETA ** (-jnp.arange(0, DIFF_HEAD_DIM, 2, dtype=jnp.float32) / DIFF_HEAD_DIM)
    invf = jnp.tile(inv_freq, LANES // half).reshape(1, LANES)
    sign = jnp.tile(jnp.concatenate([-jnp.ones(half), jnp.ones(half)]), LANES // DIFF_HEAD_DIM)
    sign = sign.reshape(1, LANES).astype(jnp.float32)
    w_qkv = od_w_qkv[0].astype(bf16)
    pos = positions.reshape(n, 1)
    q = _proj_rope(h2, w_qkv[:, :d], pos, invf, sign, batch, DIFF_HEAD_DIM ** -0.5 * LOG2_E)
    k = _proj_rope(h2, w_qkv[:, d:2 * d], pos, invf, sign, batch, 1.0)
    vt = _proj_vt(h2, w_qkv[:, 2 * d:], batch)
    o = _attention(lam, q, k, vt, od_subln_g[0].reshape(-1, 1).astype(jnp.float32), lam_init)

    w_rt = jnp.zeros((EXPERT_ROWS, d), jnp.float32).at[:N_EXPERTS].set(od_w_router[0].T)
    x3 = _out_proj(o, x2, od_w_o[0].astype(bf16))
    route, counts = _router(x3, _row(od_norm_moe[0]), w_rt)

    tm_e = TM_MOE
    rows = (n * TOP_K // tm_e + N_EXPERTS) * tm_e
    cnt = counts[:N_EXPERTS, 0].astype(jnp.int32)
    padded = (cnt + tm_e - 1) // tm_e * tm_e
    ends = jnp.cumsum(padded)
    offs = ends - padded
    idx = route[ROUTE_IDX:ROUTE_IDX + TOP_K].astype(jnp.int32)
    rank = route[ROUTE_RANK:ROUTE_RANK + TOP_K].astype(jnp.int32)
    expert_ids = jnp.arange(N_EXPERTS, dtype=jnp.int32).reshape(N_EXPERTS, 1, 1)
    dest = rank + jnp.sum(jnp.where(idx[None] == expert_ids, offs.reshape(N_EXPERTS, 1, 1), 0), axis=0)
    gates = route[ROUTE_GATE:ROUTE_GATE + TOP_K].T
    tm_r = TM_ROW
    dest3 = dest.reshape(TOP_K, n // tm_r, tm_r).transpose(1, 0, 2).reshape(n // tm_r, 1, TOP_K * tm_r)
    n_valid = (ends[-1] // tm_e).reshape(1).astype(jnp.int32)
    tile_start = jnp.arange(rows // tm_e, dtype=jnp.int32) * tm_e
    tile_expert = jnp.minimum(jnp.sum(tile_start[:, None] >= ends[None, :], axis=1), N_EXPERTS - 1).astype(jnp.int32)

    xs = _dispatch(dest3, x3, rows)
    ys = _moe_ffn(tile_expert, n_valid, xs, _row(od_norm_moe[0]), od_we_gate[0].astype(bf16),
                  od_we_up[0].astype(bf16), od_we_down[0].astype(bf16))
    out = _combine(dest3, x3, gates, _row(final_norm), ys)
    return out.reshape(batch, seq, d)
```

```python
import functools
import math

import jax
import jax.numpy as jnp
from jax import lax
from jax.experimental import pallas as pl
from jax.experimental.pallas import tpu as pltpu

EPS = 1e-5
POOL_WINDOWS = (2, 4, 8, 16)
POOL_HALO = 16
GMLP_CHUNK = 128
GMLP_GROUPS = 8
DIFF_HEAD_DIM = 64
HEAD_WIDTH = 2 * DIFF_HEAD_DIM
ROPE_THETA = 10000.0
N_EXPERTS = 8
TOP_K = 2
LANES = 128
NEG_BIG = -1e30
LOG2_E = math.log2(math.e)

VMEM_LIMIT_BYTES = 60 * 1024 * 1024

TM_MIX = 256
TM_FFN = 512
TF_FFN = 512
TM_PROJ = 512
PROJ_SPLIT = 4
TQ_ATT = 512
TK_ATT = 512
ATT_UNROLL = 4
TM_OUT = 512
TM_ROUTER = 1024
TM_MOE = 512
TM_ROW = 512
ROW_DMA_UNROLL = 8


def _cparams(sem):
    return pltpu.CompilerParams(dimension_semantics=sem, vmem_limit_bytes=VMEM_LIMIT_BYTES)


def _const_spec(shape):
    nd = len(shape)
    return pl.BlockSpec(shape, lambda *_: (0,) * nd, pipeline_mode=pl.Buffered(1))


def _rms(x, g):
    return x * lax.rsqrt(jnp.mean(x * x, axis=-1, keepdims=True) + EPS) * g


def _bdot(a, b):
    return jnp.dot(a, b, preferred_element_type=jnp.float32)


def _mixer_body(tiles_per_seq, x_ref, g_ref, w_in_ref, w_pool_ref, pscale_ref, ln_g_ref, ln_b_ref,
                w_sp_ref, b_sp_ref, w_out_ref, g_next_ref, x_out_ref, h_out_ref, ext_ref):
    tm = x_ref.shape[0]
    pool_w = w_pool_ref.shape[0] * w_pool_ref.shape[1]
    pool_g = w_pool_ref.shape[1]
    gm_w = ln_g_ref.shape[1]
    i = pl.program_id(0)
    seq_tile = i % tiles_per_seq

    x = x_ref[...]
    h = _rms(x, g_ref[...]).astype(jnp.bfloat16)
    z = _bdot(h, w_in_ref[...])

    zp = z[:, :pool_w]

    @pl.when(seq_tile == 0)
    def _():
        ext_ref[0:POOL_HALO, :] = jnp.zeros((POOL_HALO, pool_w), jnp.float32)

    ext_ref[POOL_HALO:, :] = zp
    tpos = seq_tile * tm + lax.broadcasted_iota(jnp.int32, (tm, 1), 0)
    pooled_parts = []
    for gi, w in enumerate(POOL_WINDOWS):
        cols = slice(gi * pool_g, (gi + 1) * pool_g)
        p = ext_ref[:, cols]
        k = 1
        while k < w:
            p = p + pltpu.roll(p, k, axis=0)
            k *= 2
        cnt = jnp.minimum(tpos + 1, w).astype(jnp.float32)
        pooled = p[POOL_HALO:, :] / cnt - zp[:, cols]
        pooled_parts.append(_bdot(pooled.astype(jnp.bfloat16), w_pool_ref[gi]))
    ext_ref[0:POOL_HALO, :] = zp[tm - POOL_HALO:, :]
    y_pool = jnp.concatenate(pooled_parts, axis=-1) * pscale_ref[...]

    a = z[:, pool_w:]
    a = 0.5 * a * (1.0 + lax.erf(a * (1.0 / math.sqrt(2.0))))
    u = a[:, :gm_w]
    v = a[:, gm_w:]
    mu = jnp.mean(v, axis=-1, keepdims=True)
    vc = v - mu
    var = jnp.mean(vc * vc, axis=-1, keepdims=True)
    vn = (vc * lax.rsqrt(var + EPS) * ln_g_ref[...] + ln_b_ref[...]).astype(jnp.bfloat16)
    gd = gm_w // GMLP_GROUPS
    row = lax.broadcasted_iota(jnp.int32, (GMLP_CHUNK, GMLP_CHUNK), 0)
    col = lax.broadcasted_iota(jnp.int32, (GMLP_CHUNK, GMLP_CHUNK), 1)
    tril = col <= row
    w_sp = [jnp.where(tril, w_sp_ref[g], 0.0).astype(jnp.bfloat16) for g in range(GMLP_GROUPS)]
    b_sp = b_sp_ref[...]
    rows_out = []
    for c in range(tm // GMLP_CHUNK):
        rs = slice(c * GMLP_CHUNK, (c + 1) * GMLP_CHUNK)
        mixed = jnp.concatenate(
            [_bdot(w_sp[g], vn[rs, g * gd:(g + 1) * gd]) for g in range(GMLP_GROUPS)], axis=-1)
        rows_out.append(u[rs, :] * (mixed + b_sp))
    y_gate = jnp.concatenate(rows_out, axis=0)

    y = jnp.concatenate([y_pool, y_gate], axis=-1).astype(jnp.bfloat16)
    x1 = x + _bdot(y, w_out_ref[...])
    x_out_ref[...] = x1
    h_out_ref[...] = _rms(x1, g_next_ref[...]).astype(jnp.bfloat16)


def _mixer(x, seq, g, w_in, w_pool, pscale, ln_g, ln_b, w_sp, b_sp_full, w_out, g_next):
    n, d = x.shape
    tm = TM_MIX
    assert seq % tm == 0 and tm % GMLP_CHUNK == 0 and tm >= POOL_HALO
    pool_w = w_pool.shape[0] * w_pool.shape[1]
    row_spec = pl.BlockSpec((tm, d), lambda i: (i, 0))
    return pl.pallas_call(
        functools.partial(_mixer_body, seq // tm),
        grid=(n // tm,),
        in_specs=[row_spec, _const_spec(g.shape), _const_spec(w_in.shape), _const_spec(w_pool.shape),
                  _const_spec(pscale.shape), _const_spec(ln_g.shape), _const_spec(ln_b.shape),
                  _const_spec(w_sp.shape), _const_spec(b_sp_full.shape), _const_spec(w_out.shape),
                  _const_spec(g_next.shape)],
        out_specs=[row_spec, row_spec],
        out_shape=[jax.ShapeDtypeStruct((n, d), jnp.float32), jax.ShapeDtypeStruct((n, d), jnp.bfloat16)],
        scratch_shapes=[pltpu.VMEM((tm + POOL_HALO, pool_w), jnp.float32)],
        compiler_params=_cparams(("arbitrary",)),
        name="mixer0",
    )(x, g, w_in, w_pool, pscale, ln_g, ln_b, w_sp, b_sp_full, w_out, g_next)


def _swiglu_step(h, wgu_ref, wd_ref):
    tf = wd_ref.shape[0]
    ab = _bdot(h, wgu_ref[...])
    a = ab[:, :tf]
    y = (a * (1.0 / (1.0 + jnp.exp(-a))) * ab[:, tf:]).astype(jnp.bfloat16)
    return _bdot(y, wd_ref[...])


def _dense_ffn_body(x_ref, h_ref, wgu_ref, wd_ref, g_next_ref, x_out_ref, h_out_ref, acc_ref):
    j = pl.program_id(1)

    @pl.when(j == 0)
    def _():
        acc_ref[...] = _swiglu_step(h_ref[...], wgu_ref, wd_ref)

    @pl.when(j > 0)
    def _():
        acc_ref[...] += _swiglu_step(h_ref[...], wgu_ref, wd_ref)

    @pl.when(j == pl.num_programs(1) - 1)
    def _():
        x2 = x_ref[...] + acc_ref[...]
        x_out_ref[...] = x2
        h_out_ref[...] = _rms(x2, g_next_ref[...]).astype(jnp.bfloat16)


def _moe_ffn_body(te_ref, nv_ref, x_ref, g_ref, wgu_ref, wd_ref, y_out_ref, h_ref, acc_ref):
    i = pl.program_id(0)
    j = pl.program_id(1)

    @pl.when(i < nv_ref[0])
    def _():
        @pl.when(j == 0)
        def _():
            h = _rms(x_ref[...], g_ref[...]).astype(jnp.bfloat16)
            h_ref[...] = h
            acc_ref[...] = _swiglu_step(h, wgu_ref, wd_ref)

        @pl.when(j > 0)
        def _():
            acc_ref[...] += _swiglu_step(h_ref[...], wgu_ref, wd_ref)

        @pl.when(j == pl.num_programs(1) - 1)
        def _():
            y_out_ref[...] = acc_ref[...]

    @pl.when((i >= nv_ref[0]) & (j == 0))
    def _():
        y_out_ref[...] = jnp.zeros_like(y_out_ref)


def _chunked_gate_up(wg, wu, tf):
    *lead, d, ff = wg.shape
    both = jnp.stack([wg.reshape(*lead, d, ff // tf, tf), wu.reshape(*lead, d, ff // tf, tf)], axis=-2)
    both = jnp.moveaxis(both.astype(jnp.bfloat16), -3, -4)
    return both.reshape(*lead, ff // tf, d, 2 * tf)


def _dense_ffn(x, h, wgu, wd, g_next):
    n, d = x.shape
    n_chunks, _, tf2 = wgu.shape
    tm, tf = TM_FFN, tf2 // 2
    assert n % tm == 0 and wd.shape == (n_chunks * tf, d)
    row_spec = pl.BlockSpec((tm, d), lambda i, j: (i, 0))
    return pl.pallas_call(
        _dense_ffn_body,
        grid=(n // tm, n_chunks),
        in_specs=[row_spec, row_spec, pl.BlockSpec((None, d, tf2), lambda i, j: (j, 0, 0)),
                  pl.BlockSpec((tf, d), lambda i, j: (j, 0)), pl.BlockSpec(g_next.shape, lambda i, j: (0, 0))],
        out_specs=[row_spec, row_spec],
        out_shape=[jax.ShapeDtypeStruct((n, d), jnp.float32), jax.ShapeDtypeStruct((n, d), jnp.bfloat16)],
        scratch_shapes=[pltpu.VMEM((tm, d), jnp.float32)],
        compiler_params=_cparams(("arbitrary", "arbitrary")),
        name="dense_ffn",
    )(x, h, wgu, wd, g_next)


def _moe_ffn(tile_expert, n_valid, xs, g, wgu, wd):
    r, d = xs.shape
    _, n_chunks, _, tf2 = wgu.shape
    tm, tf = TM_MOE, tf2 // 2
    assert r % tm == 0 and wd.shape[1:] == (n_chunks * tf, d)
    n_tiles = r // tm

    def row_idx(i, j, te, nv):
        return (jnp.minimum(i, nv[0] - 1), 0)

    def expert_of(i, te, nv):
        return te[jnp.minimum(i, nv[0] - 1)]

    def chunk_of(i, j, nv):
        return jnp.where(i < nv[0], j, n_chunks - 1)

    wgu_spec = pl.BlockSpec((None, None, d, tf2), lambda i, j, te, nv: (expert_of(i, te, nv), chunk_of(i, j, nv), 0, 0))
    wd_spec = pl.BlockSpec((None, tf, d), lambda i, j, te, nv: (expert_of(i, te, nv), chunk_of(i, j, nv), 0))
    return pl.pallas_call(
        _moe_ffn_body,
        grid_spec=pltpu.PrefetchScalarGridSpec(
            num_scalar_prefetch=2,
            grid=(n_tiles, n_chunks),
            in_specs=[pl.BlockSpec((tm, d), row_idx), pl.BlockSpec(g.shape, lambda i, j, *s: (0, 0)), wgu_spec, wd_spec],
            out_specs=pl.BlockSpec((tm, d), lambda i, j, *s: (i, 0)),
            scratch_shapes=[pltpu.VMEM((tm, d), jnp.bfloat16), pltpu.VMEM((tm, d), jnp.float32)],
        ),
        out_shape=jax.ShapeDtypeStruct((r, d), jnp.float32),
        compiler_params=_cparams(("arbitrary", "arbitrary")),
        name="moe_ffn",
    )(tile_expert, n_valid, xs, g, wgu, wd)


def _proj_rope_body(scale, h_ref, w_ref, pos_ref, invf_ref, sign_ref, o_ref):
    half = DIFF_HEAD_DIM // 2
    ang = pos_ref[...].astype(jnp.float32) * invf_ref[...]
    cos = jnp.cos(ang) * scale
    sin = jnp.sin(ang) * sign_ref[...] * scale
    lane = lax.broadcasted_iota(jnp.int32, (1, LANES), 1)
    first_half = (lane % DIFF_HEAD_DIM) < half
    rows = h_ref.shape[0] // PROJ_SPLIT
    for part in range(PROJ_SPLIT):
        rs = slice(part * rows, (part + 1) * rows)
        t = _bdot(h_ref[rs, :], w_ref[...])
        for hd in range(o_ref.shape[0]):
            blk = t[:, hd * HEAD_WIDTH:(hd + 1) * HEAD_WIDTH]
            partner = jnp.where(first_half, pltpu.roll(blk, LANES - half, axis=1), pltpu.roll(blk, half, axis=1))
            o_ref[hd, rs, :] = (blk * cos[rs] + partner * sin[rs]).astype(o_ref.dtype)


def _proj_rope(h, w, pos, invf, sign, batch, scale):
    n, d = h.shape
    seq = n // batch
    tm = TM_PROJ
    heads = d // HEAD_WIDTH
    tiles = seq // tm
    return pl.pallas_call(
        functools.partial(_proj_rope_body, scale),
        grid=(n // tm,),
        in_specs=[pl.BlockSpec((tm, d), lambda i: (i, 0)), _const_spec(w.shape),
                  pl.BlockSpec((tm, 1), lambda i: (i, 0)), _const_spec(invf.shape), _const_spec(sign.shape)],
        out_specs=pl.BlockSpec((None, heads, tm, HEAD_WIDTH), lambda i: (i // tiles, 0, i % tiles, 0)),
        out_shape=jax.ShapeDtypeStruct((batch, heads, seq, HEAD_WIDTH), jnp.bfloat16),
        compiler_params=_cparams(("parallel",)),
        name="proj_rope",
    )(h, w, pos, invf, sign)


def _proj_vt_body(h_ref, w_ref, o_ref):
    t = _bdot(h_ref[...], w_ref[...])
    for hd in range(o_ref.shape[0]):
        o_ref[hd, 0] = t[:, hd * HEAD_WIDTH:(hd + 1) * HEAD_WIDTH].T.astype(o_ref.dtype)


def _proj_vt(h, w, batch):
    n, d = h.shape
    seq = n // batch
    tm = TK_ATT
    heads = d // HEAD_WIDTH
    tiles = seq // tm
    return pl.pallas_call(
        _proj_vt_body,
        grid=(n // tm,),
        in_specs=[pl.BlockSpec((tm, d), lambda i: (i, 0)), _const_spec(w.shape)],
        out_specs=pl.BlockSpec((None, heads, 1, HEAD_WIDTH, tm), lambda i: (i // tiles, 0, i % tiles, 0, 0)),
        out_shape=jax.ShapeDtypeStruct((batch, heads, tiles, HEAD_WIDTH, tm), jnp.bfloat16),
        compiler_params=_cparams(("parallel",)),
        name="proj_vt",
    )(h, w)


def _attn_body(lam_init, lam_ref, q_ref, k_ref, vt_ref, g_ref, o_ref, s_ref, cmax_ref, m_ref, l_ref, acc_ref, bias_ref):
    tq = q_ref.shape[0]
    tk = vt_ref.shape[2]
    qi = pl.program_id(2)
    lane = lax.broadcasted_iota(jnp.int32, (1, HEAD_WIDTH), 1)
    q = q_ref[...]
    zero = jnp.zeros_like(q)
    qz = (jnp.where(lane < DIFF_HEAD_DIM, q, zero), jnp.where(lane >= DIFF_HEAD_DIM, q, zero))

    m_ref[...] = jnp.full_like(m_ref, NEG_BIG)
    l_ref[...] = jnp.zeros_like(l_ref)
    acc_ref[...] = jnp.zeros_like(acc_ref)

    @pl.when(qi == 0)
    def _():
        krow = lax.broadcasted_iota(jnp.int32, (tk, tq), 0)
        qcol = lax.broadcasted_iota(jnp.int32, (tk, tq), 1)
        bias_ref[...] = jnp.where(krow <= qcol, 0.0, NEG_BIG)

    def scores(kv, slot, masked, c):
        start = pl.multiple_of(kv * tk, tk)
        kb = k_ref[pl.ds(start, tk), :]
        s = lax.dot_general(kb, qz[c], (((1,), (1,)), ((), ())), preferred_element_type=jnp.float32)
        if masked:
            s = s + bias_ref[...]
        s_ref[slot, c] = s
        cmax_ref[slot, c] = jnp.max(s, axis=0, keepdims=True)

    def consume(kv, slot, c):
        m_old = m_ref[c]
        m_new = jnp.maximum(m_old, cmax_ref[slot, c])
        alpha = jnp.exp2(m_old - m_new)
        p = jnp.exp2(s_ref[slot, c] - m_new)
        l_ref[c] = alpha * l_ref[c] + jnp.sum(p, axis=0, keepdims=True)
        acc_ref[c] = alpha * acc_ref[c] + _bdot(vt_ref[kv], p.astype(jnp.bfloat16))
        m_ref[c] = m_new

    for c in range(2):
        scores(qi, 0, True, c)

    def step(j, read_slot):
        kv_cur = jnp.where(j == 0, qi, j - 1)
        for c in range(2):
            scores(j, 1 - read_slot, False, c)
            consume(kv_cur, read_slot, c)

    def main(t, carry):
        for u in range(ATT_UNROLL):
            step(ATT_UNROLL * t + u, u % 2)
        return carry

    lax.fori_loop(0, qi // ATT_UNROLL, main, 0)
    done = qi // ATT_UNROLL * ATT_UNROLL

    def pair(t, carry):
        step(done + 2 * t, 0)
        step(done + 2 * t + 1, 1)
        return carry

    lax.fori_loop(0, (qi - done) // 2, pair, 0)
    last = jnp.where(qi == 0, qi, qi - 1)

    @pl.when(qi % 2 == 1)
    def _():
        step(qi - 1, 0)
        for c in range(2):
            consume(last, 1, c)

    @pl.when(qi % 2 == 0)
    def _():
        for c in range(2):
            consume(last, 0, c)

    o = acc_ref[0] / l_ref[0] - lam_ref[0] * (acc_ref[1] / l_ref[1])
    o = o * lax.rsqrt(jnp.mean(o * o, axis=0, keepdims=True) + EPS) * g_ref[...] * (1.0 - lam_init)
    o_ref[...] = o.T.astype(o_ref.dtype)


def _attention(lam, q, k, vt, g_col, lam_init):
    batch, heads, seq, hw = q.shape
    tq, tk = TQ_ATT, TK_ATT
    assert tq == tk and seq % tq == 0
    return pl.pallas_call(
        functools.partial(_attn_body, lam_init),
        grid=(batch, heads, seq // tq),
        in_specs=[pl.BlockSpec(memory_space=pltpu.SMEM),
                  pl.BlockSpec((None, None, tq, hw), lambda b, h, i: (b, h, i, 0)),
                  pl.BlockSpec((None, None, seq, hw), lambda b, h, i: (b, h, 0, 0)),
                  pl.BlockSpec((None, None, seq // tk, hw, tk), lambda b, h, i: (b, h, 0, 0, 0)),
                  pl.BlockSpec(g_col.shape, lambda b, h, i: (0, 0))],
        out_specs=pl.BlockSpec((None, None, tq, hw), lambda b, h, i: (b, h, i, 0)),
        out_shape=jax.ShapeDtypeStruct((batch, heads, seq, hw), jnp.bfloat16),
        scratch_shapes=[pltpu.VMEM((2, 2, tk, tq), jnp.float32), pltpu.VMEM((2, 2, 1, tq), jnp.float32),
                        pltpu.VMEM((2, 1, tq), jnp.float32), pltpu.VMEM((2, 1, tq), jnp.float32),
                        pltpu.VMEM((2, hw, tq), jnp.float32), pltpu.VMEM((tk, tq), jnp.float32)],
        compiler_params=_cparams(("parallel", "parallel", "arbitrary")),
        name="diff_attn",
    )(lam, q, k, vt, g_col)


ROUTE_IDX, ROUTE_GATE, ROUTE_RANK = 0, 2, 4
ROUTE_ROWS = 8
EXPERT_ROWS = 16


def _out_proj_body(o_ref, x_ref, w_o_ref, x_out_ref):
    o = jnp.concatenate([o_ref[hd] for hd in range(o_ref.shape[0])], axis=-1)
    x_out_ref[...] = x_ref[...] + _bdot(o, w_o_ref[...])


def _out_proj(o, x, w_o):
    batch, heads, seq, hw = o.shape
    d = heads * hw
    tm = TM_OUT
    tiles = seq // tm
    row_spec = pl.BlockSpec((tm, d), lambda b, i: (b * tiles + i, 0))
    return pl.pallas_call(
        _out_proj_body,
        grid=(batch, tiles),
        in_specs=[pl.BlockSpec((None, heads, tm, hw), lambda b, i: (b, 0, i, 0)), row_spec, _const_spec(w_o.shape)],
        out_specs=row_spec,
        out_shape=jax.ShapeDtypeStruct(x.shape, jnp.float32),
        compiler_params=_cparams(("parallel", "parallel")),
        name="out_proj",
    )(o, x, w_o)


def _router_body(x_ref, g_ref, w_rt_ref, route_ref, count_ref, run_ref):
    tm = x_ref.shape[0]

    @pl.when(pl.program_id(0) == 0)
    def _():
        run_ref[...] = jnp.zeros_like(run_ref)

    h = _rms(x_ref[...], g_ref[...])
    h_hi = h.astype(jnp.bfloat16)
    h_lo = (h - h_hi.astype(jnp.float32)).astype(jnp.bfloat16)
    w_rt = w_rt_ref[...]
    w_hi = w_rt.astype(jnp.bfloat16)
    w_lo = (w_rt - w_hi.astype(jnp.float32)).astype(jnp.bfloat16)
    nt = (((1,), (1,)), ((), ()))
    logits = (lax.dot_general(w_hi, h_hi, nt, preferred_element_type=jnp.float32)
              + lax.dot_general(w_lo, h_hi, nt, preferred_element_type=jnp.float32)
              + lax.dot_general(w_hi, h_lo, nt, preferred_element_type=jnp.float32))
    row = lax.broadcasted_iota(jnp.int32, (EXPERT_ROWS, tm), 0).astype(jnp.float32)
    logits = jnp.where(row < N_EXPERTS, logits, -jnp.inf)
    m1 = jnp.max(logits, axis=0, keepdims=True)
    e1 = jnp.min(jnp.where(logits == m1, row, float(EXPERT_ROWS)), axis=0, keepdims=True)
    sel1 = row == e1
    rest = jnp.where(sel1, -jnp.inf, logits)
    m2 = jnp.max(rest, axis=0, keepdims=True)
    e2 = jnp.min(jnp.where(rest == m2, row, float(EXPERT_ROWS)), axis=0, keepdims=True)
    sel2 = row == e2
    t = jnp.exp(m2 - m1)
    w1 = 1.0 / (1.0 + t)
    w2 = t / (1.0 + t)

    onehot = (sel1 | sel2).astype(jnp.bfloat16)
    r_i = lax.broadcasted_iota(jnp.int32, (tm, tm), 0)
    c_i = lax.broadcasted_iota(jnp.int32, (tm, tm), 1)
    earlier = (r_i < c_i).astype(jnp.bfloat16)
    before = _bdot(onehot, earlier) + run_ref[:, 0:1]
    run_ref[...] += jnp.sum(onehot.astype(jnp.float32), axis=1, keepdims=True)
    rank1 = jnp.sum(jnp.where(sel1, before, 0.0), axis=0, keepdims=True)
    rank2 = jnp.sum(jnp.where(sel2, before, 0.0), axis=0, keepdims=True)

    rrow = lax.broadcasted_iota(jnp.int32, (ROUTE_ROWS, tm), 0)
    route = jnp.zeros((ROUTE_ROWS, tm), jnp.float32)
    for rw, val in ((ROUTE_IDX, e1), (ROUTE_IDX + 1, e2), (ROUTE_GATE, w1), (ROUTE_GATE + 1, w2),
                    (ROUTE_RANK, rank1), (ROUTE_RANK + 1, rank2)):
        route = jnp.where(rrow == rw, val, route)
    route_ref[...] = route
    count_ref[...] = run_ref[...]


def _router(x, g, w_rt):
    n, d = x.shape
    tm = TM_ROUTER
    return pl.pallas_call(
        _router_body,
        grid=(n // tm,),
        in_specs=[pl.BlockSpec((tm, d), lambda i: (i, 0)), _const_spec(g.shape), _const_spec(w_rt.shape)],
        out_specs=[pl.BlockSpec((ROUTE_ROWS, tm), lambda i: (0, i)),
                   pl.BlockSpec((EXPERT_ROWS, LANES), lambda i: (0, 0))],
        out_shape=[jax.ShapeDtypeStruct((ROUTE_ROWS, n), jnp.float32),
                   jax.ShapeDtypeStruct((EXPERT_ROWS, LANES), jnp.float32)],
        scratch_shapes=[pltpu.VMEM((EXPERT_ROWS, LANES), jnp.float32)],
        compiler_params=_cparams(("arbitrary",)),
        name="router",
    )(x, g, w_rt)


def _row_copy(src_ref, dst_ref, sem):
    return pltpu.make_async_copy(src_ref, dst_ref, sem)


def _dispatch_body(dest_ref, x_ref, xs_in_ref, xs_ref, sem):
    del xs_in_ref
    tm = x_ref.shape[0]

    def issue(r, c):
        for k in range(TOP_K):
            _row_copy(x_ref.at[pl.ds(r, 1)], xs_ref.at[pl.ds(dest_ref[0, k * tm + r], 1)], sem).start(priority=k)
        return c

    lax.fori_loop(0, tm, issue, 0, unroll=ROW_DMA_UNROLL)

    def drain(r, c):
        for k in range(TOP_K):
            _row_copy(x_ref.at[pl.ds(0, 1)], xs_ref.at[pl.ds(0, 1)], sem).wait()
        return c

    lax.fori_loop(0, tm, drain, 0, unroll=ROW_DMA_UNROLL)


def _dispatch(dest3, x, rows):
    n, d = x.shape
    tm = TM_ROW
    zeros = jnp.zeros((rows, d), x.dtype)
    return pl.pallas_call(
        _dispatch_body,
        grid=(n // tm,),
        in_specs=[pl.BlockSpec((None, 1, TOP_K * tm), lambda i: (i, 0, 0), memory_space=pltpu.SMEM),
                  pl.BlockSpec((tm, d), lambda i: (i, 0)),
                  pl.BlockSpec(memory_space=pl.ANY)],
        out_specs=pl.BlockSpec(memory_space=pl.ANY),
        out_shape=jax.ShapeDtypeStruct((rows, d), x.dtype),
        scratch_shapes=[pltpu.SemaphoreType.DMA],
        input_output_aliases={2: 0},
        compiler_params=_cparams(("arbitrary",)),
        name="dispatch",
    )(dest3, x, zeros)


def _combine_body(dest_ref, x_ref, gate_ref, g_ref, ys_ref, o_ref, buf_ref, sem):
    tm = x_ref.shape[0]

    def issue(r, c):
        for k in range(TOP_K):
            _row_copy(ys_ref.at[pl.ds(dest_ref[0, k * tm + r], 1)], buf_ref.at[k, pl.ds(r, 1)], sem).start(priority=k)
        return c

    lax.fori_loop(0, tm, issue, 0, unroll=ROW_DMA_UNROLL)

    def drain(r, c):
        for k in range(TOP_K):
            _row_copy(ys_ref.at[pl.ds(0, 1)], buf_ref.at[k, pl.ds(0, 1)], sem).wait()
        return c

    lax.fori_loop(0, tm, drain, 0, unroll=ROW_DMA_UNROLL)

    gates = gate_ref[...]
    y = x_ref[...] + gates[:, 0:1] * buf_ref[0] + gates[:, 1:2] * buf_ref[1]
    o_ref[...] = _rms(y, g_ref[...])


def _combine(dest3, x, gates, g, ys):
    n, d = x.shape
    tm = TM_ROW
    return pl.pallas_call(
        _combine_body,
        grid=(n // tm,),
        in_specs=[pl.BlockSpec((None, 1, TOP_K * tm), lambda i: (i, 0, 0), memory_space=pltpu.SMEM),
                  pl.BlockSpec((tm, d), lambda i: (i, 0)),
                  pl.BlockSpec((tm, TOP_K), lambda i: (i, 0)),
                  _const_spec(g.shape),
                  pl.BlockSpec(memory_space=pl.ANY)],
        out_specs=pl.BlockSpec((tm, d), lambda i: (i, 0)),
        out_shape=jax.ShapeDtypeStruct((n, d), jnp.float32),
        scratch_shapes=[pltpu.VMEM((TOP_K, tm, d), jnp.float32), pltpu.SemaphoreType.DMA],
        compiler_params=_cparams(("arbitrary",)),
        name="combine",
    )(dest3, x, gates, g, ys)


def _row(v):
    return v.reshape(1, -1).astype(jnp.float32)


def kernel(x, positions, ev_norm_mix, ev_w_in, ev_w_pool, ev_pool_scale, ev_ln_g, ev_ln_b, ev_w_spatial, ev_b_spatial, ev_w_out, ev_norm_ffn, ev_w_gate, ev_w_up, ev_w_down, od_norm_attn, od_w_qkv, od_lam_q1, od_lam_k1, od_lam_q2, od_lam_k2, od_subln_g, od_w_o, od_norm_moe, od_w_router, od_we_gate, od_we_up, od_we_down, final_norm):
    assert ev_norm_mix.shape[0] == 1 and od_norm_attn.shape[0] == 1, "one even and one odd layer"
    batch, seq, d = x.shape
    n = batch * seq
    bf16 = jnp.bfloat16
    gm_w = ev_ln_g.shape[-1]

    b_sp_full = jnp.repeat(ev_b_spatial[0].T, gm_w // GMLP_GROUPS, axis=1)
    x1, h1 = _mixer(x.reshape(n, d), seq, _row(ev_norm_mix[0]), ev_w_in[0].astype(bf16), ev_w_pool[0].astype(bf16),
                    _row(ev_pool_scale[0]), _row(ev_ln_g[0]), _row(ev_ln_b[0]), ev_w_spatial[0], b_sp_full,
                    ev_w_out[0].astype(bf16), _row(ev_norm_ffn[0]))
    x2, h2 = _dense_ffn(x1, h1, _chunked_gate_up(ev_w_gate[0], ev_w_up[0], TF_FFN), ev_w_down[0].astype(bf16),
                        _row(od_norm_attn[0]))

    lam_init = 0.8 - 0.6 * math.exp(-0.3 * 1)
    lam = (jnp.exp(jnp.sum(od_lam_q1[0] * od_lam_k1[0])) - jnp.exp(jnp.sum(od_lam_q2[0] * od_lam_k2[0]))
           + lam_init).reshape(1).astype(jnp.float32)
    half = DIFF_HEAD_DIM // 2
    inv_freq = ROPE_THETA ** (-jnp.arange(0, DIFF_HEAD_DIM, 2, dtype=jnp.float32) / DIFF_HEAD_DIM)
    invf = jnp.tile(inv_freq, LANES // half).reshape(1, LANES)
    sign = jnp.tile(jnp.concatenate([-jnp.ones(half), jnp.ones(half)]), LANES // DIFF_HEAD_DIM)
    sign = sign.reshape(1, LANES).astype(jnp.float32)
    w_qkv = od_w_qkv[0].astype(bf16)
    pos = positions.reshape(n, 1)
    q = _proj_rope(h2, w_qkv[:, :d], pos, invf, sign, batch, DIFF_HEAD_DIM ** -0.5 * LOG2_E)
    k = _proj_rope(h2, w_qkv[:, d:2 * d], pos, invf, sign, batch, 1.0)
    vt = _proj_vt(h2, w_qkv[:, 2 * d:], batch)
    o = _attention(lam, q, k, vt, od_subln_g[0].reshape(-1, 1).astype(jnp.float32), lam_init)

    w_rt = jnp.zeros((EXPERT_ROWS, d), jnp.float32).at[:N_EXPERTS].set(od_w_router[0].T)
    x3 = _out_proj(o, x2, od_w_o[0].astype(bf16))
    route, counts = _router(x3, _row(od_norm_moe[0]), w_rt)

    tm_e = TM_MOE
    rows = (n * TOP_K // tm_e + N_EXPERTS) * tm_e
    cnt = counts[:N_EXPERTS, 0].astype(jnp.int32)
    padded = (cnt + tm_e - 1) // tm_e * tm_e
    ends = jnp.cumsum(padded)
    offs = ends - padded
    idx = route[ROUTE_IDX:ROUTE_IDX + TOP_K].astype(jnp.int32)
    rank = route[ROUTE_RANK:ROUTE_RANK + TOP_K].astype(jnp.int32)
    expert_ids = jnp.arange(N_EXPERTS, dtype=jnp.int32).reshape(N_EXPERTS, 1, 1)
    dest = rank + jnp.sum(jnp.where(idx[None] == expert_ids, offs.reshape(N_EXPERTS, 1, 1), 0), axis=0)
    gates = route[ROUTE_GATE:ROUTE_GATE + TOP_K].T
    tm_r = TM_ROW
    dest3 = dest.reshape(TOP_K, n // tm_r, tm_r).transpose(1, 0, 2).reshape(n // tm_r, 1, TOP_K * tm_r)
    n_valid = (ends[-1] // tm_e).reshape(1).astype(jnp.int32)
    tile_start = jnp.arange(rows // tm_e, dtype=jnp.int32) * tm_e
    tile_expert = jnp.minimum(jnp.sum(tile_start[:, None] >= ends[None, :], axis=1), N_EXPERTS - 1).astype(jnp.int32)

    xs = _dispatch(dest3, x3, rows)
    ys = _moe_ffn(tile_expert, n_valid, xs, _row(od_norm_moe[0]),
                  _chunked_gate_up(od_we_gate[0], od_we_up[0], TF_FFN), od_we_down[0].astype(bf16))
    out = _combine(dest3, x3, gates, _row(final_norm), ys)
    return out.reshape(batch, seq, d)
```

```python
import functools
import math

import jax
import jax.numpy as jnp
from jax import lax
from jax.experimental import pallas as pl
from jax.experimental.pallas import tpu as pltpu

EPS = 1e-5
POOL_WINDOWS = (2, 4, 8, 16)
POOL_HALO = 16
GMLP_CHUNK = 128
GMLP_GROUPS = 8
DIFF_HEAD_DIM = 64
HEAD_WIDTH = 2 * DIFF_HEAD_DIM
ROPE_THETA = 10000.0
N_EXPERTS = 8
TOP_K = 2
LANES = 128
NEG_BIG = -1e30
LOG2_E = math.log2(math.e)

VMEM_LIMIT_BYTES = 60 * 1024 * 1024

TM_MIX = 512
MIX_SPLIT = 2
TM_FFN = 512
TF_FFN = 512
TM_PROJ = 512
PROJ_SPLIT = 4
TQ_ATT = 512
TK_ATT = 512
ATT_UNROLL = 4
TM_OUT = 512
TM_ROUTER = 1024
TM_MOE = 512
TM_ROW = 512
ROW_DMA_UNROLL = 8


def _cparams(sem):
    return pltpu.CompilerParams(dimension_semantics=sem, vmem_limit_bytes=VMEM_LIMIT_BYTES)


def _const_spec(shape):
    nd = len(shape)
    return pl.BlockSpec(shape, lambda *_: (0,) * nd, pipeline_mode=pl.Buffered(1))


def _rms(x, g):
    return x * lax.rsqrt(jnp.mean(x * x, axis=-1, keepdims=True) + EPS) * g


def _bdot(a, b):
    return jnp.dot(a, b, preferred_element_type=jnp.float32)


def _mixer_body(tiles_per_seq, x_ref, g_ref, w_in_ref, w_pool_ref, pscale_ref, ln_g_ref, ln_b_ref,
                w_sp_ref, b_sp_ref, w_out_ref, g_next_ref, x_out_ref, h_out_ref, ext_ref):
    tm = x_ref.shape[0]
    rows = tm // MIX_SPLIT
    pool_w = w_pool_ref.shape[0] * w_pool_ref.shape[1]
    pool_g = w_pool_ref.shape[1]
    gm_w = ln_g_ref.shape[1]
    gd = gm_w // GMLP_GROUPS
    i = pl.program_id(0)
    seq_tile = i % tiles_per_seq

    @pl.when(seq_tile == 0)
    def _():
        ext_ref[0:POOL_HALO, :] = jnp.zeros((POOL_HALO, pool_w), jnp.float32)

    r_i = lax.broadcasted_iota(jnp.int32, (GMLP_CHUNK, GMLP_CHUNK), 0)
    c_i = lax.broadcasted_iota(jnp.int32, (GMLP_CHUNK, GMLP_CHUNK), 1)
    tril = c_i <= r_i
    w_sp = [jnp.where(tril, w_sp_ref[g], 0.0).astype(jnp.bfloat16) for g in range(GMLP_GROUPS)]
    b_sp = b_sp_ref[...]

    for part in range(MIX_SPLIT):
        rs = slice(part * rows, (part + 1) * rows)
        x = x_ref[rs, :]
        h = _rms(x, g_ref[...]).astype(jnp.bfloat16)
        z = _bdot(h, w_in_ref[...])

        zp = z[:, :pool_w]
        ext_ref[POOL_HALO:, :] = zp
        tpos = seq_tile * tm + part * rows + lax.broadcasted_iota(jnp.int32, (rows, 1), 0)
        pooled_parts = []
        for gi, w in enumerate(POOL_WINDOWS):
            cols = slice(gi * pool_g, (gi + 1) * pool_g)
            p = ext_ref[:, cols]
            k = 1
            while k < w:
                p = p + pltpu.roll(p, k, axis=0)
                k *= 2
            cnt = jnp.minimum(tpos + 1, w).astype(jnp.float32)
            pooled = p[POOL_HALO:, :] / cnt - zp[:, cols]
            pooled_parts.append(_bdot(pooled.astype(jnp.bfloat16), w_pool_ref[gi]))
        ext_ref[0:POOL_HALO, :] = zp[rows - POOL_HALO:, :]
        y_pool = jnp.concatenate(pooled_parts, axis=-1) * pscale_ref[...]

        a = z[:, pool_w:]
        a = 0.5 * a * (1.0 + lax.erf(a * (1.0 / math.sqrt(2.0))))
        u = a[:, :gm_w]
        v = a[:, gm_w:]
        mu = jnp.mean(v, axis=-1, keepdims=True)
        vc = v - mu
        var = jnp.mean(vc * vc, axis=-1, keepdims=True)
        vn = (vc * lax.rsqrt(var + EPS) * ln_g_ref[...] + ln_b_ref[...]).astype(jnp.bfloat16)
        rows_out = []
        for c in range(rows // GMLP_CHUNK):
            cs = slice(c * GMLP_CHUNK, (c + 1) * GMLP_CHUNK)
            mixed = jnp.concatenate(
                [_bdot(w_sp[g], vn[cs, g * gd:(g + 1) * gd]) for g in range(GMLP_GROUPS)], axis=-1)
            rows_out.append(u[cs, :] * (mixed + b_sp))
        y_gate = jnp.concatenate(rows_out, axis=0)

        y = jnp.concatenate([y_pool, y_gate], axis=-1).astype(jnp.bfloat16)
        x1 = x + _bdot(y, w_out_ref[...])
        x_out_ref[rs, :] = x1
        h_out_ref[rs, :] = _rms(x1, g_next_ref[...]).astype(jnp.bfloat16)


def _mixer(x, seq, g, w_in, w_pool, pscale, ln_g, ln_b, w_sp, b_sp_full, w_out, g_next):
    n, d = x.shape
    tm = TM_MIX
    rows = tm // MIX_SPLIT
    assert seq % tm == 0 and rows % GMLP_CHUNK == 0 and rows >= POOL_HALO
    pool_w = w_pool.shape[0] * w_pool.shape[1]
    row_spec = pl.BlockSpec((tm, d), lambda i: (i, 0))
    return pl.pallas_call(
        functools.partial(_mixer_body, seq // tm),
        grid=(n // tm,),
        in_specs=[row_spec, _const_spec(g.shape), _const_spec(w_in.shape), _const_spec(w_pool.shape),
                  _const_spec(pscale.shape), _const_spec(ln_g.shape), _const_spec(ln_b.shape),
                  _const_spec(w_sp.shape), _const_spec(b_sp_full.shape), _const_spec(w_out.shape),
                  _const_spec(g_next.shape)],
        out_specs=[row_spec, row_spec],
        out_shape=[jax.ShapeDtypeStruct((n, d), jnp.float32), jax.ShapeDtypeStruct((n, d), jnp.bfloat16)],
        scratch_shapes=[pltpu.VMEM((rows + POOL_HALO, pool_w), jnp.float32)],
        compiler_params=_cparams(("arbitrary",)),
        name="mixer0",
    )(x, g, w_in, w_pool, pscale, ln_g, ln_b, w_sp, b_sp_full, w_out, g_next)


def _swiglu_step(h, wg_ref, wu_ref, wd_ref):
    a = _bdot(h, wg_ref[...])
    b = _bdot(h, wu_ref[...])
    y = (a * (1.0 / (1.0 + jnp.exp(-a))) * b).astype(jnp.bfloat16)
    return _bdot(y, wd_ref[...])


def _dense_ffn_body(te_ref, x_ref, h_ref, wg_ref, wu_ref, wd_ref, g_next_ref, x_out_ref, h_out_ref, acc_ref):
    j = pl.program_id(1)

    @pl.when(j == 0)
    def _():
        acc_ref[...] = _swiglu_step(h_ref[...], wg_ref, wu_ref, wd_ref)

    @pl.when(j > 0)
    def _():
        acc_ref[...] += _swiglu_step(h_ref[...], wg_ref, wu_ref, wd_ref)

    @pl.when(j == pl.num_programs(1) - 1)
    def _():
        x2 = x_ref[...] + acc_ref[...]
        x_out_ref[...] = x2
        h_out_ref[...] = _rms(x2, g_next_ref[...]).astype(jnp.bfloat16)


def _moe_ffn_body(te_ref, nv_ref, x_ref, g_ref, wg_ref, wu_ref, wd_ref, y_out_ref, h_ref, acc_ref):
    i = pl.program_id(0)
    j = pl.program_id(1)

    @pl.when(i < nv_ref[0])
    def _():
        @pl.when(j == 0)
        def _():
            h = _rms(x_ref[...], g_ref[...]).astype(jnp.bfloat16)
            h_ref[...] = h
            acc_ref[...] = _swiglu_step(h, wg_ref, wu_ref, wd_ref)

        @pl.when(j > 0)
        def _():
            acc_ref[...] += _swiglu_step(h_ref[...], wg_ref, wu_ref, wd_ref)

        @pl.when(j == pl.num_programs(1) - 1)
        def _():
            y_out_ref[...] = acc_ref[...]

    @pl.when((i >= nv_ref[0]) & (j == 0))
    def _():
        y_out_ref[...] = jnp.zeros_like(y_out_ref)


def _weight_specs(d, ff, tf, expert_of):
    wg_spec = pl.BlockSpec((None, d, tf), lambda i, j, *s: (expert_of(i, *s), 0, j))
    wd_spec = pl.BlockSpec((None, tf, d), lambda i, j, *s: (expert_of(i, *s), j, 0))
    return wg_spec, wg_spec, wd_spec


def _dense_ffn(x, h, wg, wu, wd, g_next):
    n, d = x.shape
    ff = wg.shape[-1]
    tm, tf = TM_FFN, TF_FFN
    assert n % tm == 0 and ff % tf == 0
    row_spec = pl.BlockSpec((tm, d), lambda i, j, *s: (i, 0))
    zero = jnp.zeros((1,), jnp.int32)
    return pl.pallas_call(
        _dense_ffn_body,
        grid_spec=pltpu.PrefetchScalarGridSpec(
            num_scalar_prefetch=1,
            grid=(n // tm, ff // tf),
            in_specs=[row_spec, row_spec, *_weight_specs(d, ff, tf, lambda i, te: te[0]),
                      pl.BlockSpec(g_next.shape, lambda i, j, *s: (0, 0))],
            out_specs=[row_spec, row_spec],
            scratch_shapes=[pltpu.VMEM((tm, d), jnp.float32)],
        ),
        out_shape=[jax.ShapeDtypeStruct((n, d), jnp.float32), jax.ShapeDtypeStruct((n, d), jnp.bfloat16)],
        compiler_params=_cparams(("arbitrary", "arbitrary")),
        name="dense_ffn",
    )(zero, x, h, wg, wu, wd, g_next)


def _moe_ffn(tile_expert, n_valid, xs, g, wg, wu, wd):
    r, d = xs.shape
    ff = wg.shape[-1]
    tm, tf = TM_MOE, TF_FFN
    assert r % tm == 0 and ff % tf == 0
    n_tiles = r // tm

    def row_idx(i, j, te, nv):
        return (jnp.minimum(i, nv[0] - 1), 0)

    def expert_of(i, te, nv):
        return te[jnp.minimum(i, nv[0] - 1)]

    def ff_idx(i, j, nv):
        return jnp.where(i < nv[0], j, ff // tf - 1)

    wg_spec = pl.BlockSpec((None, d, tf), lambda i, j, te, nv: (expert_of(i, te, nv), 0, ff_idx(i, j, nv)))
    wd_spec = pl.BlockSpec((None, tf, d), lambda i, j, te, nv: (expert_of(i, te, nv), ff_idx(i, j, nv), 0))
    row_spec = pl.BlockSpec((tm, d), row_idx)
    return pl.pallas_call(
        _moe_ffn_body,
        grid_spec=pltpu.PrefetchScalarGridSpec(
            num_scalar_prefetch=2,
            grid=(n_tiles, ff // tf),
            in_specs=[row_spec, pl.BlockSpec(g.shape, lambda i, j, *s: (0, 0)), wg_spec, wg_spec, wd_spec],
            out_specs=pl.BlockSpec((tm, d), lambda i, j, *s: (i, 0)),
            scratch_shapes=[pltpu.VMEM((tm, d), jnp.bfloat16), pltpu.VMEM((tm, d), jnp.float32)],
        ),
        out_shape=jax.ShapeDtypeStruct((r, d), jnp.float32),
        compiler_params=_cparams(("arbitrary", "arbitrary")),
        name="moe_ffn",
    )(tile_expert, n_valid, xs, g, wg, wu, wd)


def _proj_rope_body(scale, h_ref, w_ref, pos_ref, invf_ref, sign_ref, o_ref):
    half = DIFF_HEAD_DIM // 2
    ang = pos_ref[...].astype(jnp.float32) * invf_ref[...]
    cos = jnp.cos(ang) * scale
    sin = jnp.sin(ang) * sign_ref[...] * scale
    lane = lax.broadcasted_iota(jnp.int32, (1, LANES), 1)
    first_half = (lane % DIFF_HEAD_DIM) < half
    rows = h_ref.shape[0] // PROJ_SPLIT
    for part in range(PROJ_SPLIT):
        rs = slice(part * rows, (part + 1) * rows)
        t = _bdot(h_ref[rs, :], w_ref[...])
        for hd in range(o_ref.shape[0]):
            blk = t[:, hd * HEAD_WIDTH:(hd + 1) * HEAD_WIDTH]
            partner = jnp.where(first_half, pltpu.roll(blk, LANES - half, axis=1), pltpu.roll(blk, half, axis=1))
            o_ref[hd, rs, :] = (blk * cos[rs] + partner * sin[rs]).astype(o_ref.dtype)


def _proj_rope(h, w, pos, invf, sign, batch, scale):
    n, d = h.shape
    seq = n // batch
    tm = TM_PROJ
    heads = d // HEAD_WIDTH
    tiles = seq // tm
    return pl.pallas_call(
        functools.partial(_proj_rope_body, scale),
        grid=(n // tm,),
        in_specs=[pl.BlockSpec((tm, d), lambda i: (i, 0)), _const_spec(w.shape),
                  pl.BlockSpec((tm, 1), lambda i: (i, 0)), _const_spec(invf.shape), _const_spec(sign.shape)],
        out_specs=pl.BlockSpec((None, heads, tm, HEAD_WIDTH), lambda i: (i // tiles, 0, i % tiles, 0)),
        out_shape=jax.ShapeDtypeStruct((batch, heads, seq, HEAD_WIDTH), jnp.bfloat16),
        compiler_params=_cparams(("parallel",)),
        name="proj_rope",
    )(h, w, pos, invf, sign)


def _proj_vt_body(h_ref, w_ref, o_ref):
    t = _bdot(h_ref[...], w_ref[...])
    for hd in range(o_ref.shape[0]):
        o_ref[hd, 0] = t[:, hd * HEAD_WIDTH:(hd + 1) * HEAD_WIDTH].T.astype(o_ref.dtype)


def _proj_vt(h, w, batch):
    n, d = h.shape
    seq = n // batch
    tm = TK_ATT
    heads = d // HEAD_WIDTH
    tiles = seq // tm
    return pl.pallas_call(
        _proj_vt_body,
        grid=(n // tm,),
        in_specs=[pl.BlockSpec((tm, d), lambda i: (i, 0)), _const_spec(w.shape)],
        out_specs=pl.BlockSpec((None, heads, 1, HEAD_WIDTH, tm), lambda i: (i // tiles, 0, i % tiles, 0, 0)),
        out_shape=jax.ShapeDtypeStruct((batch, heads, tiles, HEAD_WIDTH, tm), jnp.bfloat16),
        compiler_params=_cparams(("parallel",)),
        name="proj_vt",
    )(h, w)


def _attn_body(lam_init, lam_ref, q_ref, k_ref, vt_ref, g_ref, o_ref, s_ref, cmax_ref, m_ref, l_ref, acc_ref, bias_ref):
    tq = q_ref.shape[0]
    tk = vt_ref.shape[2]
    qi = pl.program_id(2)
    lane = lax.broadcasted_iota(jnp.int32, (1, HEAD_WIDTH), 1)
    q = q_ref[...]
    zero = jnp.zeros_like(q)
    qz = (jnp.where(lane < DIFF_HEAD_DIM, q, zero), jnp.where(lane >= DIFF_HEAD_DIM, q, zero))

    m_ref[...] = jnp.full_like(m_ref, NEG_BIG)
    l_ref[...] = jnp.zeros_like(l_ref)
    acc_ref[...] = jnp.zeros_like(acc_ref)

    @pl.when(qi == 0)
    def _():
        krow = lax.broadcasted_iota(jnp.int32, (tk, tq), 0)
        qcol = lax.broadcasted_iota(jnp.int32, (tk, tq), 1)
        bias_ref[...] = jnp.where(krow <= qcol, 0.0, NEG_BIG)

    def scores(kv, slot, masked, c):
        start = pl.multiple_of(kv * tk, tk)
        kb = k_ref[pl.ds(start, tk), :]
        s = lax.dot_general(kb, qz[c], (((1,), (1,)), ((), ())), preferred_element_type=jnp.float32)
        if masked:
            s = s + bias_ref[...]
        s_ref[slot, c] = s
        cmax_ref[slot, c] = jnp.max(s, axis=0, keepdims=True)

    def consume(kv, slot, c):
        m_old = m_ref[c]
        m_new = jnp.maximum(m_old, cmax_ref[slot, c])
        alpha = jnp.exp2(m_old - m_new)
        p = jnp.exp2(s_ref[slot, c] - m_new)
        l_ref[c] = alpha * l_ref[c] + jnp.sum(p, axis=0, keepdims=True)
        acc_ref[c] = alpha * acc_ref[c] + _bdot(vt_ref[kv], p.astype(jnp.bfloat16))
        m_ref[c] = m_new

    for c in range(2):
        scores(qi, 0, True, c)

    def step(j, read_slot):
        kv_cur = jnp.where(j == 0, qi, j - 1)
        for c in range(2):
            scores(j, 1 - read_slot, False, c)
            consume(kv_cur, read_slot, c)

    def main(t, carry):
        for u in range(ATT_UNROLL):
            step(ATT_UNROLL * t + u, u % 2)
        return carry

    lax.fori_loop(0, qi // ATT_UNROLL, main, 0)
    done = qi // ATT_UNROLL * ATT_UNROLL

    def pair(t, carry):
        step(done + 2 * t, 0)
        step(done + 2 * t + 1, 1)
        return carry

    lax.fori_loop(0, (qi - done) // 2, pair, 0)
    last = jnp.where(qi == 0, qi, qi - 1)

    @pl.when(qi % 2 == 1)
    def _():
        step(qi - 1, 0)
        for c in range(2):
            consume(last, 1, c)

    @pl.when(qi % 2 == 0)
    def _():
        for c in range(2):
            consume(last, 0, c)

    o = acc_ref[0] / l_ref[0] - lam_ref[0] * (acc_ref[1] / l_ref[1])
    o = o * lax.rsqrt(jnp.mean(o * o, axis=0, keepdims=True) + EPS) * g_ref[...] * (1.0 - lam_init)
    o_ref[...] = o.T.astype(o_ref.dtype)


def _attention(lam, q, k, vt, g_col, lam_init):
    batch, heads, seq, hw = q.shape
    tq, tk = TQ_ATT, TK_ATT
    assert tq == tk and seq % tq == 0
    return pl.pallas_call(
        functools.partial(_attn_body, lam_init),
        grid=(batch, heads, seq // tq),
        in_specs=[pl.BlockSpec(memory_space=pltpu.SMEM),
                  pl.BlockSpec((None, None, tq, hw), lambda b, h, i: (b, h, i, 0)),
                  pl.BlockSpec((None, None, seq, hw), lambda b, h, i: (b, h, 0, 0)),
                  pl.BlockSpec((None, None, seq // tk, hw, tk), lambda b, h, i: (b, h, 0, 0, 0)),
                  pl.BlockSpec(g_col.shape, lambda b, h, i: (0, 0))],
        out_specs=pl.BlockSpec((None, None, tq, hw), lambda b, h, i: (b, h, i, 0)),
        out_shape=jax.ShapeDtypeStruct((batch, heads, seq, hw), jnp.bfloat16),
        scratch_shapes=[pltpu.VMEM((2, 2, tk, tq), jnp.float32), pltpu.VMEM((2, 2, 1, tq), jnp.float32),
                        pltpu.VMEM((2, 1, tq), jnp.float32), pltpu.VMEM((2, 1, tq), jnp.float32),
                        pltpu.VMEM((2, hw, tq), jnp.float32), pltpu.VMEM((tk, tq), jnp.float32)],
        compiler_params=_cparams(("parallel", "parallel", "arbitrary")),
        name="diff_attn",
    )(lam, q, k, vt, g_col)


ROUTE_IDX, ROUTE_GATE, ROUTE_RANK = 0, 2, 4
ROUTE_ROWS = 8
EXPERT_ROWS = 16


def _out_proj_body(o_ref, x_ref, w_o_ref, x_out_ref):
    o = jnp.concatenate([o_ref[hd] for hd in range(o_ref.shape[0])], axis=-1)
    x_out_ref[...] = x_ref[...] + _bdot(o, w_o_ref[...])


def _out_proj(o, x, w_o):
    batch, heads, seq, hw = o.shape
    d = heads * hw
    tm = TM_OUT
    tiles = seq // tm
    row_spec = pl.BlockSpec((tm, d), lambda b, i: (b * tiles + i, 0))
    return pl.pallas_call(
        _out_proj_body,
        grid=(batch, tiles),
        in_specs=[pl.BlockSpec((None, heads, tm, hw), lambda b, i: (b, 0, i, 0)), row_spec, _const_spec(w_o.shape)],
        out_specs=row_spec,
        out_shape=jax.ShapeDtypeStruct(x.shape, jnp.float32),
        compiler_params=_cparams(("parallel", "parallel")),
        name="out_proj",
    )(o, x, w_o)


def _router_body(x_ref, g_ref, w_rt_ref, route_ref, count_ref, run_ref):
    tm = x_ref.shape[0]

    @pl.when(pl.program_id(0) == 0)
    def _():
        run_ref[...] = jnp.zeros_like(run_ref)

    h = _rms(x_ref[...], g_ref[...])
    h_hi = h.astype(jnp.bfloat16)
    h_lo = (h - h_hi.astype(jnp.float32)).astype(jnp.bfloat16)
    w_rt = w_rt_ref[...]
    w_hi = w_rt.astype(jnp.bfloat16)
    w_lo = (w_rt - w_hi.astype(jnp.float32)).astype(jnp.bfloat16)
    nt = (((1,), (1,)), ((), ()))
    logits = (lax.dot_general(w_hi, h_hi, nt, preferred_element_type=jnp.float32)
              + lax.dot_general(w_lo, h_hi, nt, preferred_element_type=jnp.float32)
              + lax.dot_general(w_hi, h_lo, nt, preferred_element_type=jnp.float32))
    row = lax.broadcasted_iota(jnp.int32, (EXPERT_ROWS, tm), 0).astype(jnp.float32)
    logits = jnp.where(row < N_EXPERTS, logits, -jnp.inf)
    m1 = jnp.max(logits, axis=0, keepdims=True)
    e1 = jnp.min(jnp.where(logits == m1, row, float(EXPERT_ROWS)), axis=0, keepdims=True)
    sel1 = row == e1
    rest = jnp.where(sel1, -jnp.inf, logits)
    m2 = jnp.max(rest, axis=0, keepdims=True)
    e2 = jnp.min(jnp.where(rest == m2, row, float(EXPERT_ROWS)), axis=0, keepdims=True)
    sel2 = row == e2
    t = jnp.exp(m2 - m1)
    w1 = 1.0 / (1.0 + t)
    w2 = t / (1.0 + t)

    onehot = (sel1 | sel2).astype(jnp.bfloat16)
    r_i = lax.broadcasted_iota(jnp.int32, (tm, tm), 0)
    c_i = lax.broadcasted_iota(jnp.int32, (tm, tm), 1)
    earlier = (r_i < c_i).astype(jnp.bfloat16)
    before = _bdot(onehot, earlier) + run_ref[:, 0:1]
    run_ref[...] += jnp.sum(onehot.astype(jnp.float32), axis=1, keepdims=True)
    rank1 = jnp.sum(jnp.where(sel1, before, 0.0), axis=0, keepdims=True)
    rank2 = jnp.sum(jnp.where(sel2, before, 0.0), axis=0, keepdims=True)

    rrow = lax.broadcasted_iota(jnp.int32, (ROUTE_ROWS, tm), 0)
    route = jnp.zeros((ROUTE_ROWS, tm), jnp.float32)
    for rw, val in ((ROUTE_IDX, e1), (ROUTE_IDX + 1, e2), (ROUTE_GATE, w1), (ROUTE_GATE + 1, w2),
                    (ROUTE_RANK, rank1), (ROUTE_RANK + 1, rank2)):
        route = jnp.where(rrow == rw, val, route)
    route_ref[...] = route
    count_ref[...] = run_ref[...]


def _router(x, g, w_rt):
    n, d = x.shape
    tm = TM_ROUTER
    return pl.pallas_call(
        _router_body,
        grid=(n // tm,),
        in_specs=[pl.BlockSpec((tm, d), lambda i: (i, 0)), _const_spec(g.shape), _const_spec(w_rt.shape)],
        out_specs=[pl.BlockSpec((ROUTE_ROWS, tm), lambda i: (0, i)),
                   pl.BlockSpec((EXPERT_ROWS, LANES), lambda i: (0, 0))],
        out_shape=[jax.ShapeDtypeStruct((ROUTE_ROWS, n), jnp.float32),
                   jax.ShapeDtypeStruct((EXPERT_ROWS, LANES), jnp.float32)],
        scratch_shapes=[pltpu.VMEM((EXPERT_ROWS, LANES), jnp.float32)],
        compiler_params=_cparams(("arbitrary",)),
        name="router",
    )(x, g, w_rt)


def _row_copy(src_ref, dst_ref, sem):
    return pltpu.make_async_copy(src_ref, dst_ref, sem)


def _dispatch_body(fill_ref, dest_ref, x_ref, xs_ref, sem):
    tm = x_ref.shape[0]

    @pl.when(pl.program_id(0) == 0)
    def _():
        def fill(r, c):
            @pl.when(fill_ref[r] >= 0)
            def _():
                _row_copy(x_ref.at[pl.ds(0, 1)], xs_ref.at[pl.ds(fill_ref[r], 1)], sem).start()
            return c

        lax.fori_loop(0, fill_ref.shape[0], fill, 0)

        def fill_drain(r, c):
            @pl.when(fill_ref[r] >= 0)
            def _():
                _row_copy(x_ref.at[pl.ds(0, 1)], xs_ref.at[pl.ds(0, 1)], sem).wait()
            return c

        lax.fori_loop(0, fill_ref.shape[0], fill_drain, 0)

    def issue(r, c):
        for k in range(TOP_K):
            _row_copy(x_ref.at[pl.ds(r, 1)], xs_ref.at[pl.ds(dest_ref[0, k * tm + r], 1)], sem).start(priority=k)
        return c

    lax.fori_loop(0, tm, issue, 0, unroll=ROW_DMA_UNROLL)

    def drain(r, c):
        for k in range(TOP_K):
            _row_copy(x_ref.at[pl.ds(0, 1)], xs_ref.at[pl.ds(0, 1)], sem).wait()
        return c

    lax.fori_loop(0, tm, drain, 0, unroll=ROW_DMA_UNROLL)


def _dispatch(fill_rows, dest3, x, rows):
    n, d = x.shape
    tm = TM_ROW
    return pl.pallas_call(
        _dispatch_body,
        grid_spec=pltpu.PrefetchScalarGridSpec(
            num_scalar_prefetch=1,
            grid=(n // tm,),
            in_specs=[pl.BlockSpec((None, 1, TOP_K * tm), lambda i, *s: (i, 0, 0), memory_space=pltpu.SMEM),
                      pl.BlockSpec((tm, d), lambda i, *s: (i, 0))],
            out_specs=pl.BlockSpec(memory_space=pl.ANY),
            scratch_shapes=[pltpu.SemaphoreType.DMA],
        ),
        out_shape=jax.ShapeDtypeStruct((rows, d), x.dtype),
        compiler_params=_cparams(("arbitrary",)),
        name="dispatch",
    )(fill_rows, dest3, x)


def _combine_body(dest_ref, x_ref, gate_ref, g_ref, ys_ref, o_ref, buf_ref, sem):
    tm = x_ref.shape[0]

    def issue(r, c):
        for k in range(TOP_K):
            _row_copy(ys_ref.at[pl.ds(dest_ref[0, k * tm + r], 1)], buf_ref.at[k, pl.ds(r, 1)], sem).start(priority=k)
        return c

    lax.fori_loop(0, tm, issue, 0, unroll=ROW_DMA_UNROLL)

    def drain(r, c):
        for k in range(TOP_K):
            _row_copy(ys_ref.at[pl.ds(0, 1)], buf_ref.at[k, pl.ds(0, 1)], sem).wait()
        return c

    lax.fori_loop(0, tm, drain, 0, unroll=ROW_DMA_UNROLL)

    gates = gate_ref[...]
    y = x_ref[...] + gates[:, 0:1] * buf_ref[0] + gates[:, 1:2] * buf_ref[1]
    o_ref[...] = _rms(y, g_ref[...])


def _combine(dest3, x, gates, g, ys):
    n, d = x.shape
    tm = TM_ROW
    return pl.pallas_call(
        _combine_body,
        grid=(n // tm,),
        in_specs=[pl.BlockSpec((None, 1, TOP_K * tm), lambda i: (i, 0, 0), memory_space=pltpu.SMEM),
                  pl.BlockSpec((tm, d), lambda i: (i, 0)),
                  pl.BlockSpec((tm, TOP_K), lambda i: (i, 0)),
                  _const_spec(g.shape),
                  pl.BlockSpec(memory_space=pl.ANY)],
        out_specs=pl.BlockSpec((tm, d), lambda i: (i, 0)),
        out_shape=jax.ShapeDtypeStruct((n, d), jnp.float32),
        scratch_shapes=[pltpu.VMEM((TOP_K, tm, d), jnp.float32), pltpu.SemaphoreType.DMA],
        compiler_params=_cparams(("arbitrary",)),
        name="combine",
    )(dest3, x, gates, g, ys)


def _row(v):
    return v.reshape(1, -1).astype(jnp.float32)


def kernel(x, positions, ev_norm_mix, ev_w_in, ev_w_pool, ev_pool_scale, ev_ln_g, ev_ln_b, ev_w_spatial, ev_b_spatial, ev_w_out, ev_norm_ffn, ev_w_gate, ev_w_up, ev_w_down, od_norm_attn, od_w_qkv, od_lam_q1, od_lam_k1, od_lam_q2, od_lam_k2, od_subln_g, od_w_o, od_norm_moe, od_w_router, od_we_gate, od_we_up, od_we_down, final_norm):
    assert ev_norm_mix.shape[0] == 1 and od_norm_attn.shape[0] == 1, "one even and one odd layer"
    batch, seq, d = x.shape
    n = batch * seq
    bf16 = jnp.bfloat16
    gm_w = ev_ln_g.shape[-1]

    b_sp_full = jnp.repeat(ev_b_spatial[0].T, gm_w // GMLP_GROUPS, axis=1)
    x1, h1 = _mixer(x.reshape(n, d), seq, _row(ev_norm_mix[0]), ev_w_in[0].astype(bf16), ev_w_pool[0].astype(bf16),
                    _row(ev_pool_scale[0]), _row(ev_ln_g[0]), _row(ev_ln_b[0]), ev_w_spatial[0], b_sp_full,
                    ev_w_out[0].astype(bf16), _row(ev_norm_ffn[0]))
    x2, h2 = _dense_ffn(x1, h1, ev_w_gate.astype(bf16), ev_w_up.astype(bf16), ev_w_down.astype(bf16),
                        _row(od_norm_attn[0]))

    lam_init = 0.8 - 0.6 * math.exp(-0.3 * 1)
    lam = (jnp.exp(jnp.sum(od_lam_q1[0] * od_lam_k1[0])) - jnp.exp(jnp.sum(od_lam_q2[0] * od_lam_k2[0]))
           + lam_init).reshape(1).astype(jnp.float32)
    half = DIFF_HEAD_DIM // 2
    inv_freq = ROPE_THETA ** (-jnp.arange(0, DIFF_HEAD_DIM, 2, dtype=jnp.float32) / DIFF_HEAD_DIM)
    invf = jnp.tile(inv_freq, LANES // half).reshape(1, LANES)
    sign = jnp.tile(jnp.concatenate([-jnp.ones(half), jnp.ones(half)]), LANES // DIFF_HEAD_DIM)
    sign = sign.reshape(1, LANES).astype(jnp.float32)
    w_qkv = od_w_qkv[0].astype(bf16)
    pos = positions.reshape(n, 1)
    q = _proj_rope(h2, w_qkv[:, :d], pos, invf, sign, batch, DIFF_HEAD_DIM ** -0.5 * LOG2_E)
    k = _proj_rope(h2, w_qkv[:, d:2 * d], pos, invf, sign, batch, 1.0)
    vt = _proj_vt(h2, w_qkv[:, 2 * d:], batch)
    o = _attention(lam, q, k, vt, od_subln_g[0].reshape(-1, 1).astype(jnp.float32), lam_init)

    w_rt = jnp.zeros((EXPERT_ROWS, d), jnp.float32).at[:N_EXPERTS].set(od_w_router[0].T)
    x3 = _out_proj(o, x2, od_w_o[0].astype(bf16))
    route, counts = _router(x3, _row(od_norm_moe[0]), w_rt)

    tm_e = TM_MOE
    rows = (n * TOP_K // tm_e + N_EXPERTS) * tm_e
    cnt = counts[:N_EXPERTS, 0].astype(jnp.int32)
    padded = (cnt + tm_e - 1) // tm_e * tm_e
    ends = jnp.cumsum(padded)
    offs = ends - padded
    idx = route[ROUTE_IDX:ROUTE_IDX + TOP_K].astype(jnp.int32)
    rank = route[ROUTE_RANK:ROUTE_RANK + TOP_K].astype(jnp.int32)
    expert_ids = jnp.arange(N_EXPERTS, dtype=jnp.int32).reshape(N_EXPERTS, 1, 1)
    dest = rank + jnp.sum(jnp.where(idx[None] == expert_ids, offs.reshape(N_EXPERTS, 1, 1), 0), axis=0)
    gates = route[ROUTE_GATE:ROUTE_GATE + TOP_K].T
    tm_r = TM_ROW
    dest3 = dest.reshape(TOP_K, n // tm_r, tm_r).transpose(1, 0, 2).reshape(n // tm_r, 1, TOP_K * tm_r)
    n_valid = (ends[-1] // tm_e).reshape(1).astype(jnp.int32)
    tile_start = jnp.arange(rows // tm_e, dtype=jnp.int32) * tm_e
    tile_expert = jnp.minimum(jnp.sum(tile_start[:, None] >= ends[None, :], axis=1), N_EXPERTS - 1).astype(jnp.int32)

    within = jnp.arange(tm_e, dtype=jnp.int32)[None, :]
    pad_rows = jnp.where(within < (padded - cnt)[:, None], (offs + cnt)[:, None] + within, -1)
    tail = ends[-1] + jnp.arange(N_EXPERTS * tm_e, dtype=jnp.int32)
    fill_rows = jnp.concatenate([pad_rows.reshape(-1), jnp.where(tail < rows, tail, -1)])

    xs = _dispatch(fill_rows, dest3, x3, rows)
    ys = _moe_ffn(tile_expert, n_valid, xs, _row(od_norm_moe[0]), od_we_gate[0].astype(bf16),
                  od_we_up[0].astype(bf16), od_we_down[0].astype(bf16))
    out = _combine(dest3, x3, gates, _row(final_norm), ys)
    return out.reshape(batch, seq, d)
```

```python
import functools
import math

import jax
import jax.numpy as jnp
from jax import lax
from jax.experimental import pallas as pl
from jax.experimental.pallas import tpu as pltpu

EPS = 1e-5
POOL_WINDOWS = (2, 4, 8, 16)
POOL_HALO = 16
GMLP_CHUNK = 128
GMLP_GROUPS = 8
DIFF_HEAD_DIM = 64
HEAD_WIDTH = 2 * DIFF_HEAD_DIM
ROPE_THETA = 10000.0
N_EXPERTS = 8
TOP_K = 2
LANES = 128
NEG_BIG = -1e30
LOG2_E = math.log2(math.e)

VMEM_LIMIT_BYTES = 60 * 1024 * 1024

TM_MIX = 512
MIX_SPLIT = 2
TM_FFN = 512
TF_FFN = 512
TM_PROJ = 512
PROJ_SPLIT = 4
TQ_ATT = 512
TK_ATT = 512
ATT_UNROLL = 4
TM_OUT = 512
TM_ROUTER = 1024
TM_MOE = 512
TM_ROW = 512
ROW_DMA_UNROLL = 8


def _cparams(sem):
    return pltpu.CompilerParams(dimension_semantics=sem, vmem_limit_bytes=VMEM_LIMIT_BYTES)


def _const_spec(shape):
    nd = len(shape)
    return pl.BlockSpec(shape, lambda *_: (0,) * nd, pipeline_mode=pl.Buffered(1))


def _rms(x, g):
    return x * lax.rsqrt(jnp.mean(x * x, axis=-1, keepdims=True) + EPS) * g


def _bdot(a, b):
    return jnp.dot(a, b, preferred_element_type=jnp.float32)


def _mixer_body(tiles_per_seq, x_ref, g_ref, w_in_ref, w_pool_ref, pscale_ref, ln_g_ref, ln_b_ref,
                w_sp_ref, b_sp_ref, w_out_ref, g_next_ref, x_out_ref, h_out_ref, ext_ref):
    tm = x_ref.shape[0]
    rows = tm // MIX_SPLIT
    pool_w = w_pool_ref.shape[0] * w_pool_ref.shape[1]
    pool_g = w_pool_ref.shape[1]
    gm_w = ln_g_ref.shape[1]
    gd = gm_w // GMLP_GROUPS
    i = pl.program_id(0)
    seq_tile = i % tiles_per_seq

    @pl.when(seq_tile == 0)
    def _():
        ext_ref[0:POOL_HALO, :] = jnp.zeros((POOL_HALO, pool_w), jnp.float32)

    r_i = lax.broadcasted_iota(jnp.int32, (GMLP_CHUNK, GMLP_CHUNK), 0)
    c_i = lax.broadcasted_iota(jnp.int32, (GMLP_CHUNK, GMLP_CHUNK), 1)
    tril = c_i <= r_i
    w_sp = [jnp.where(tril, w_sp_ref[g], 0.0).astype(jnp.bfloat16) for g in range(GMLP_GROUPS)]
    b_sp = b_sp_ref[...]

    for part in range(MIX_SPLIT):
        rs = slice(part * rows, (part + 1) * rows)
        x = x_ref[rs, :]
        h = _rms(x, g_ref[...]).astype(jnp.bfloat16)
        z = _bdot(h, w_in_ref[...])

        zp = z[:, :pool_w]
        ext_ref[POOL_HALO:, :] = zp
        tpos = seq_tile * tm + part * rows + lax.broadcasted_iota(jnp.int32, (rows, 1), 0)
        pooled_parts = []
        for gi, w in enumerate(POOL_WINDOWS):
            cols = slice(gi * pool_g, (gi + 1) * pool_g)
            p = ext_ref[:, cols]
            k = 1
            while k < w:
                p = p + pltpu.roll(p, k, axis=0)
                k *= 2
            cnt = jnp.minimum(tpos + 1, w).astype(jnp.float32)
            pooled = p[POOL_HALO:, :] / cnt - zp[:, cols]
            pooled_parts.append(_bdot(pooled.astype(jnp.bfloat16), w_pool_ref[gi]))
        ext_ref[0:POOL_HALO, :] = zp[rows - POOL_HALO:, :]
        y_pool = jnp.concatenate(pooled_parts, axis=-1) * pscale_ref[...]

        a = z[:, pool_w:]
        a = 0.5 * a * (1.0 + lax.erf(a * (1.0 / math.sqrt(2.0))))
        u = a[:, :gm_w]
        v = a[:, gm_w:]
        mu = jnp.mean(v, axis=-1, keepdims=True)
        vc = v - mu
        var = jnp.mean(vc * vc, axis=-1, keepdims=True)
        vn = (vc * lax.rsqrt(var + EPS) * ln_g_ref[...] + ln_b_ref[...]).astype(jnp.bfloat16)
        rows_out = []
        for c in range(rows // GMLP_CHUNK):
            cs = slice(c * GMLP_CHUNK, (c + 1) * GMLP_CHUNK)
            mixed = jnp.concatenate(
                [_bdot(w_sp[g], vn[cs, g * gd:(g + 1) * gd]) for g in range(GMLP_GROUPS)], axis=-1)
            rows_out.append(u[cs, :] * (mixed + b_sp))
        y_gate = jnp.concatenate(rows_out, axis=0)

        y = jnp.concatenate([y_pool, y_gate], axis=-1).astype(jnp.bfloat16)
        x1 = x + _bdot(y, w_out_ref[...])
        x_out_ref[rs, :] = x1
        h_out_ref[rs, :] = _rms(x1, g_next_ref[...]).astype(jnp.bfloat16)


def _mixer(x, seq, g, w_in, w_pool, pscale, ln_g, ln_b, w_sp, b_sp_full, w_out, g_next):
    n, d = x.shape
    tm = TM_MIX
    rows = tm // MIX_SPLIT
    assert seq % tm == 0 and rows % GMLP_CHUNK == 0 and rows >= POOL_HALO
    pool_w = w_pool.shape[0] * w_pool.shape[1]
    row_spec = pl.BlockSpec((tm, d), lambda i: (i, 0))
    return pl.pallas_call(
        functools.partial(_mixer_body, seq // tm),
        grid=(n // tm,),
        in_specs=[row_spec, _const_spec(g.shape), _const_spec(w_in.shape), _const_spec(w_pool.shape),
                  _const_spec(pscale.shape), _const_spec(ln_g.shape), _const_spec(ln_b.shape),
                  _const_spec(w_sp.shape), _const_spec(b_sp_full.shape), _const_spec(w_out.shape),
                  _const_spec(g_next.shape)],
        out_specs=[row_spec, row_spec],
        out_shape=[jax.ShapeDtypeStruct((n, d), jnp.float32), jax.ShapeDtypeStruct((n, d), jnp.bfloat16)],
        scratch_shapes=[pltpu.VMEM((rows + POOL_HALO, pool_w), jnp.float32)],
        compiler_params=_cparams(("arbitrary",)),
        name="mixer0",
    )(x, g, w_in, w_pool, pscale, ln_g, ln_b, w_sp, b_sp_full, w_out, g_next)


def _swiglu_step(h, wg_ref, wu_ref, wd_ref):
    a = _bdot(h, wg_ref[...])
    b = _bdot(h, wu_ref[...])
    y = (a * (1.0 / (1.0 + jnp.exp(-a))) * b).astype(jnp.bfloat16)
    return _bdot(y, wd_ref[...])


def _dense_ffn_body(te_ref, x_ref, h_ref, wg_ref, wu_ref, wd_ref, g_next_ref, x_out_ref, h_out_ref, acc_ref):
    j = pl.program_id(1)

    @pl.when(j == 0)
    def _():
        acc_ref[...] = _swiglu_step(h_ref[...], wg_ref, wu_ref, wd_ref)

    @pl.when(j > 0)
    def _():
        acc_ref[...] += _swiglu_step(h_ref[...], wg_ref, wu_ref, wd_ref)

    @pl.when(j == pl.num_programs(1) - 1)
    def _():
        x2 = x_ref[...] + acc_ref[...]
        x_out_ref[...] = x2
        h_out_ref[...] = _rms(x2, g_next_ref[...]).astype(jnp.bfloat16)


def _moe_ffn_body(te_ref, nv_ref, x_ref, g_ref, wg_ref, wu_ref, wd_ref, y_out_ref, h_ref, acc_ref):
    i = pl.program_id(0)
    j = pl.program_id(1)

    @pl.when(i < nv_ref[0])
    def _():
        @pl.when(j == 0)
        def _():
            h = _rms(x_ref[...], g_ref[...]).astype(jnp.bfloat16)
            h_ref[...] = h
            acc_ref[...] = _swiglu_step(h, wg_ref, wu_ref, wd_ref)

        @pl.when(j > 0)
        def _():
            acc_ref[...] += _swiglu_step(h_ref[...], wg_ref, wu_ref, wd_ref)

        @pl.when(j == pl.num_programs(1) - 1)
        def _():
            y_out_ref[...] = acc_ref[...]

    @pl.when((i >= nv_ref[0]) & (j == 0))
    def _():
        y_out_ref[...] = jnp.zeros_like(y_out_ref)


def _weight_specs(d, ff, tf, expert_of):
    wg_spec = pl.BlockSpec((None, d, tf), lambda i, j, *s: (expert_of(i, *s), 0, j))
    wd_spec = pl.BlockSpec((None, tf, d), lambda i, j, *s: (expert_of(i, *s), j, 0))
    return wg_spec, wg_spec, wd_spec


def _dense_ffn(x, h, wg, wu, wd, g_next):
    n, d = x.shape
    ff = wg.shape[-1]
    tm, tf = TM_FFN, TF_FFN
    assert n % tm == 0 and ff % tf == 0
    row_spec = pl.BlockSpec((tm, d), lambda i, j, *s: (i, 0))
    zero = jnp.zeros((1,), jnp.int32)
    return pl.pallas_call(
        _dense_ffn_body,
        grid_spec=pltpu.PrefetchScalarGridSpec(
            num_scalar_prefetch=1,
            grid=(n // tm, ff // tf),
            in_specs=[row_spec, row_spec, *_weight_specs(d, ff, tf, lambda i, te: te[0]),
                      pl.BlockSpec(g_next.shape, lambda i, j, *s: (0, 0))],
            out_specs=[row_spec, row_spec],
            scratch_shapes=[pltpu.VMEM((tm, d), jnp.float32)],
        ),
        out_shape=[jax.ShapeDtypeStruct((n, d), jnp.float32), jax.ShapeDtypeStruct((n, d), jnp.bfloat16)],
        compiler_params=_cparams(("arbitrary", "arbitrary")),
        name="dense_ffn",
    )(zero, x, h, wg, wu, wd, g_next)


def _moe_ffn(tile_expert, n_valid, xs, g, wg, wu, wd):
    r, d = xs.shape
    ff = wg.shape[-1]
    tm, tf = TM_MOE, TF_FFN
    assert r % tm == 0 and ff % tf == 0
    n_tiles = r // tm

    def row_idx(i, j, te, nv):
        return (jnp.minimum(i, nv[0] - 1), 0)

    def expert_of(i, te, nv):
        return te[jnp.minimum(i, nv[0] - 1)]

    def ff_idx(i, j, nv):
        return jnp.where(i < nv[0], j, ff // tf - 1)

    wg_spec = pl.BlockSpec((None, d, tf), lambda i, j, te, nv: (expert_of(i, te, nv), 0, ff_idx(i, j, nv)))
    wd_spec = pl.BlockSpec((None, tf, d), lambda i, j, te, nv: (expert_of(i, te, nv), ff_idx(i, j, nv), 0))
    row_spec = pl.BlockSpec((tm, d), row_idx)
    return pl.pallas_call(
        _moe_ffn_body,
        grid_spec=pltpu.PrefetchScalarGridSpec(
            num_scalar_prefetch=2,
            grid=(n_tiles, ff // tf),
            in_specs=[row_spec, pl.BlockSpec(g.shape, lambda i, j, *s: (0, 0)), wg_spec, wg_spec, wd_spec],
            out_specs=pl.BlockSpec((tm, d), lambda i, j, *s: (i, 0)),
            scratch_shapes=[pltpu.VMEM((tm, d), jnp.bfloat16), pltpu.VMEM((tm, d), jnp.float32)],
        ),
        out_shape=jax.ShapeDtypeStruct((r, d), jnp.float32),
        compiler_params=_cparams(("arbitrary", "arbitrary")),
        name="moe_ffn",
    )(tile_expert, n_valid, xs, g, wg, wu, wd)


def _proj_rope_body(scale, h_ref, w_ref, pos_ref, invf_ref, sign_ref, o_ref):
    half = DIFF_HEAD_DIM // 2
    ang = pos_ref[...].astype(jnp.float32) * invf_ref[...]
    cos = jnp.cos(ang) * scale
    sin = jnp.sin(ang) * sign_ref[...] * scale
    lane = lax.broadcasted_iota(jnp.int32, (1, LANES), 1)
    first_half = (lane % DIFF_HEAD_DIM) < half
    rows = h_ref.shape[0] // PROJ_SPLIT
    for part in range(PROJ_SPLIT):
        rs = slice(part * rows, (part + 1) * rows)
        t = _bdot(h_ref[rs, :], w_ref[...])
        for hd in range(o_ref.shape[0]):
            blk = t[:, hd * HEAD_WIDTH:(hd + 1) * HEAD_WIDTH]
            partner = jnp.where(first_half, pltpu.roll(blk, LANES - half, axis=1), pltpu.roll(blk, half, axis=1))
            o_ref[hd, rs, :] = (blk * cos[rs] + partner * sin[rs]).astype(o_ref.dtype)


def _proj_rope(h, w, pos, invf, sign, batch, scale):
    n, d = h.shape
    seq = n // batch
    tm = TM_PROJ
    heads = d // HEAD_WIDTH
    tiles = seq // tm
    return pl.pallas_call(
        functools.partial(_proj_rope_body, scale),
        grid=(n // tm,),
        in_specs=[pl.BlockSpec((tm, d), lambda i: (i, 0)), _const_spec(w.shape),
                  pl.BlockSpec((tm, 1), lambda i: (i, 0)), _const_spec(invf.shape), _const_spec(sign.shape)],
        out_specs=pl.BlockSpec((None, heads, tm, HEAD_WIDTH), lambda i: (i // tiles, 0, i % tiles, 0)),
        out_shape=jax.ShapeDtypeStruct((batch, heads, seq, HEAD_WIDTH), jnp.bfloat16),
        compiler_params=_cparams(("parallel",)),
        name="proj_rope",
    )(h, w, pos, invf, sign)


def _proj_vt_body(h_ref, w_ref, o_ref):
    t = _bdot(h_ref[...], w_ref[...])
    for hd in range(o_ref.shape[0]):
        o_ref[hd, 0] = t[:, hd * HEAD_WIDTH:(hd + 1) * HEAD_WIDTH].T.astype(o_ref.dtype)


def _proj_vt(h, w, batch):
    n, d = h.shape
    seq = n // batch
    tm = TK_ATT
    heads = d // HEAD_WIDTH
    tiles = seq // tm
    return pl.pallas_call(
        _proj_vt_body,
        grid=(n // tm,),
        in_specs=[pl.BlockSpec((tm, d), lambda i: (i, 0)), _const_spec(w.shape)],
        out_specs=pl.BlockSpec((None, heads, 1, HEAD_WIDTH, tm), lambda i: (i // tiles, 0, i % tiles, 0, 0)),
        out_shape=jax.ShapeDtypeStruct((batch, heads, tiles, HEAD_WIDTH, tm), jnp.bfloat16),
        compiler_params=_cparams(("parallel",)),
        name="proj_vt",
    )(h, w)


def _attn_body(lam_init, lam_ref, q_ref, k_ref, vt_ref, g_ref, o_ref, s_ref, cmax_ref, m_ref, l_ref, acc_ref, bias_ref):
    tq = q_ref.shape[0]
    tk = vt_ref.shape[2]
    qi = pl.program_id(2)
    lane = lax.broadcasted_iota(jnp.int32, (1, HEAD_WIDTH), 1)
    q = q_ref[...]
    zero = jnp.zeros_like(q)
    qz = (jnp.where(lane < DIFF_HEAD_DIM, q, zero), jnp.where(lane >= DIFF_HEAD_DIM, q, zero))

    m_ref[...] = jnp.full_like(m_ref, NEG_BIG)
    l_ref[...] = jnp.zeros_like(l_ref)
    acc_ref[...] = jnp.zeros_like(acc_ref)

    @pl.when(qi == 0)
    def _():
        krow = lax.broadcasted_iota(jnp.int32, (tk, tq), 0)
        qcol = lax.broadcasted_iota(jnp.int32, (tk, tq), 1)
        bias_ref[...] = jnp.where(krow <= qcol, 0.0, NEG_BIG)

    def scores(kv, slot, masked, c):
        start = pl.multiple_of(kv * tk, tk)
        kb = k_ref[pl.ds(start, tk), :]
        s = lax.dot_general(kb, qz[c], (((1,), (1,)), ((), ())), preferred_element_type=jnp.float32)
        if masked:
            s = s + bias_ref[...]
        s_ref[slot, c] = s
        cmax_ref[slot, c] = jnp.max(s, axis=0, keepdims=True)

    def consume(kv, slot, c):
        m_old = m_ref[c]
        m_new = jnp.maximum(m_old, cmax_ref[slot, c])
        alpha = jnp.exp2(m_old - m_new)
        p = jnp.exp2(s_ref[slot, c] - m_new)
        l_ref[c] = alpha * l_ref[c] + jnp.sum(p, axis=0, keepdims=True)
        acc_ref[c] = alpha * acc_ref[c] + _bdot(vt_ref[kv], p.astype(jnp.bfloat16))
        m_ref[c] = m_new

    for c in range(2):
        scores(qi, 0, True, c)

    def step(j, read_slot):
        kv_cur = jnp.where(j == 0, qi, j - 1)
        for c in range(2):
            scores(j, 1 - read_slot, False, c)
            consume(kv_cur, read_slot, c)

    def main(t, carry):
        for u in range(ATT_UNROLL):
            step(ATT_UNROLL * t + u, u % 2)
        return carry

    lax.fori_loop(0, qi // ATT_UNROLL, main, 0)
    done = qi // ATT_UNROLL * ATT_UNROLL

    def pair(t, carry):
        step(done + 2 * t, 0)
        step(done + 2 * t + 1, 1)
        return carry

    lax.fori_loop(0, (qi - done) // 2, pair, 0)
    last = jnp.where(qi == 0, qi, qi - 1)

    @pl.when(qi % 2 == 1)
    def _():
        step(qi - 1, 0)
        for c in range(2):
            consume(last, 1, c)

    @pl.when(qi % 2 == 0)
    def _():
        for c in range(2):
            consume(last, 0, c)

    o = acc_ref[0] / l_ref[0] - lam_ref[0] * (acc_ref[1] / l_ref[1])
    o = o * lax.rsqrt(jnp.mean(o * o, axis=0, keepdims=True) + EPS) * g_ref[...] * (1.0 - lam_init)
    o_ref[...] = o.T.astype(o_ref.dtype)


def _attention(lam, q, k, vt, g_col, lam_init):
    batch, heads, seq, hw = q.shape
    tq, tk = TQ_ATT, TK_ATT
    assert tq == tk and seq % tq == 0
    return pl.pallas_call(
        functools.partial(_attn_body, lam_init),
        grid=(batch, heads, seq // tq),
        in_specs=[pl.BlockSpec(memory_space=pltpu.SMEM),
                  pl.BlockSpec((None, None, tq, hw), lambda b, h, i: (b, h, i, 0)),
                  pl.BlockSpec((None, None, seq, hw), lambda b, h, i: (b, h, 0, 0)),
                  pl.BlockSpec((None, None, seq // tk, hw, tk), lambda b, h, i: (b, h, 0, 0, 0)),
                  pl.BlockSpec(g_col.shape, lambda b, h, i: (0, 0))],
        out_specs=pl.BlockSpec((None, None, tq, hw), lambda b, h, i: (b, h, i, 0)),
        out_shape=jax.ShapeDtypeStruct((batch, heads, seq, hw), jnp.bfloat16),
        scratch_shapes=[pltpu.VMEM((2, 2, tk, tq), jnp.float32), pltpu.VMEM((2, 2, 1, tq), jnp.float32),
                        pltpu.VMEM((2, 1, tq), jnp.float32), pltpu.VMEM((2, 1, tq), jnp.float32),
                        pltpu.VMEM((2, hw, tq), jnp.float32), pltpu.VMEM((tk, tq), jnp.float32)],
        compiler_params=_cparams(("parallel", "parallel", "arbitrary")),
        name="diff_attn",
    )(lam, q, k, vt, g_col)


ROUTE_IDX, ROUTE_GATE, ROUTE_RANK = 0, 2, 4
ROUTE_ROWS = 8
EXPERT_ROWS = 16


def _out_proj_body(o_ref, x_ref, w_o_ref, x_out_ref):
    o = jnp.concatenate([o_ref[hd] for hd in range(o_ref.shape[0])], axis=-1)
    x_out_ref[...] = x_ref[...] + _bdot(o, w_o_ref[...])


def _out_proj(o, x, w_o):
    batch, heads, seq, hw = o.shape
    d = heads * hw
    tm = TM_OUT
    tiles = seq // tm
    row_spec = pl.BlockSpec((tm, d), lambda b, i: (b * tiles + i, 0))
    return pl.pallas_call(
        _out_proj_body,
        grid=(batch, tiles),
        in_specs=[pl.BlockSpec((None, heads, tm, hw), lambda b, i: (b, 0, i, 0)), row_spec, _const_spec(w_o.shape)],
        out_specs=row_spec,
        out_shape=jax.ShapeDtypeStruct(x.shape, jnp.float32),
        compiler_params=_cparams(("parallel", "parallel")),
        name="out_proj",
    )(o, x, w_o)


def _router_body(x_ref, g_ref, w_rt_ref, route_ref, count_ref, run_ref):
    tm = x_ref.shape[0]

    @pl.when(pl.program_id(0) == 0)
    def _():
        run_ref[...] = jnp.zeros_like(run_ref)

    h = _rms(x_ref[...], g_ref[...])
    h_hi = h.astype(jnp.bfloat16)
    h_lo = (h - h_hi.astype(jnp.float32)).astype(jnp.bfloat16)
    w_rt = w_rt_ref[...]
    w_hi = w_rt.astype(jnp.bfloat16)
    w_lo = (w_rt - w_hi.astype(jnp.float32)).astype(jnp.bfloat16)
    nt = (((1,), (1,)), ((), ()))
    logits = (lax.dot_general(w_hi, h_hi, nt, preferred_element_type=jnp.float32)
              + lax.dot_general(w_lo, h_hi, nt, preferred_element_type=jnp.float32)
              + lax.dot_general(w_hi, h_lo, nt, preferred_element_type=jnp.float32))
    row = lax.broadcasted_iota(jnp.int32, (EXPERT_ROWS, tm), 0).astype(jnp.float32)
    logits = jnp.where(row < N_EXPERTS, logits, -jnp.inf)
    m1 = jnp.max(logits, axis=0, keepdims=True)
    e1 = jnp.min(jnp.where(logits == m1, row, float(EXPERT_ROWS)), axis=0, keepdims=True)
    sel1 = row == e1
    rest = jnp.where(sel1, -jnp.inf, logits)
    m2 = jnp.max(rest, axis=0, keepdims=True)
    e2 = jnp.min(jnp.where(rest == m2, row, float(EXPERT_ROWS)), axis=0, keepdims=True)
    sel2 = row == e2
    t = jnp.exp(m2 - m1)
    w1 = 1.0 / (1.0 + t)
    w2 = t / (1.0 + t)

    onehot = (sel1 | sel2).astype(jnp.bfloat16)
    r_i = lax.broadcasted_iota(jnp.int32, (tm, tm), 0)
    c_i = lax.broadcasted_iota(jnp.int32, (tm, tm), 1)
    earlier = (r_i < c_i).astype(jnp.bfloat16)
    before = _bdot(onehot, earlier) + run_ref[:, 0:1]
    run_ref[...] += jnp.sum(onehot.astype(jnp.float32), axis=1, keepdims=True)
    rank1 = jnp.sum(jnp.where(sel1, before, 0.0), axis=0, keepdims=True)
    rank2 = jnp.sum(jnp.where(sel2, before, 0.0), axis=0, keepdims=True)

    rrow = lax.broadcasted_iota(jnp.int32, (ROUTE_ROWS, tm), 0)
    route = jnp.zeros((ROUTE_ROWS, tm), jnp.float32)
    for rw, val in ((ROUTE_IDX, e1), (ROUTE_IDX + 1, e2), (ROUTE_GATE, w1), (ROUTE_GATE + 1, w2),
                    (ROUTE_RANK, rank1), (ROUTE_RANK + 1, rank2)):
        route = jnp.where(rrow == rw, val, route)
    route_ref[...] = route
    count_ref[...] = run_ref[...]


def _router(x, g, w_rt):
    n, d = x.shape
    tm = TM_ROUTER
    return pl.pallas_call(
        _router_body,
        grid=(n // tm,),
        in_specs=[pl.BlockSpec((tm, d), lambda i: (i, 0)), _const_spec(g.shape), _const_spec(w_rt.shape)],
        out_specs=[pl.BlockSpec((ROUTE_ROWS, tm), lambda i: (0, i)),
                   pl.BlockSpec((EXPERT_ROWS, LANES), lambda i: (0, 0))],
        out_shape=[jax.ShapeDtypeStruct((ROUTE_ROWS, n), jnp.float32),
                   jax.ShapeDtypeStruct((EXPERT_ROWS, LANES), jnp.float32)],
        scratch_shapes=[pltpu.VMEM((EXPERT_ROWS, LANES), jnp.float32)],
        compiler_params=_cparams(("arbitrary",)),
        name="router",
    )(x, g, w_rt)


def _row_copy(src_ref, dst_ref, sem):
    return pltpu.make_async_copy(src_ref, dst_ref, sem)


def _dispatch_body(dest_ref, x_ref, xs_in_ref, xs_ref, sem):
    del xs_in_ref
    tm = x_ref.shape[0]

    def issue(r, c):
        for k in range(TOP_K):
            _row_copy(x_ref.at[pl.ds(r, 1)], xs_ref.at[pl.ds(dest_ref[0, k * tm + r], 1)], sem).start(priority=k)
        return c

    lax.fori_loop(0, tm, issue, 0, unroll=ROW_DMA_UNROLL)

    def drain(r, c):
        for k in range(TOP_K):
            _row_copy(x_ref.at[pl.ds(0, 1)], xs_ref.at[pl.ds(0, 1)], sem).wait()
        return c

    lax.fori_loop(0, tm, drain, 0, unroll=ROW_DMA_UNROLL)


def _dispatch(dest3, x, rows):
    n, d = x.shape
    tm = TM_ROW
    zeros = jnp.zeros((rows, d), x.dtype)
    return pl.pallas_call(
        _dispatch_body,
        grid=(n // tm,),
        in_specs=[pl.BlockSpec((None, 1, TOP_K * tm), lambda i: (i, 0, 0), memory_space=pltpu.SMEM),
                  pl.BlockSpec((tm, d), lambda i: (i, 0)),
                  pl.BlockSpec(memory_space=pl.ANY)],
        out_specs=pl.BlockSpec(memory_space=pl.ANY),
        out_shape=jax.ShapeDtypeStruct((rows, d), x.dtype),
        scratch_shapes=[pltpu.SemaphoreType.DMA],
        input_output_aliases={2: 0},
        compiler_params=_cparams(("arbitrary",)),
        name="dispatch",
    )(dest3, x, zeros)


def _combine_body(dest_ref, x_ref, gate_ref, g_ref, ys_ref, o_ref, buf_ref, sem):
    tm = x_ref.shape[0]

    def issue(r, c):
        for k in range(TOP_K):
            _row_copy(ys_ref.at[pl.ds(dest_ref[0, k * tm + r], 1)], buf_ref.at[k, pl.ds(r, 1)], sem).start(priority=k)
        return c

    lax.fori_loop(0, tm, issue, 0, unroll=ROW_DMA_UNROLL)

    def drain(r, c):
        for k in range(TOP_K):
            _row_copy(ys_ref.at[pl.ds(0, 1)], buf_ref.at[k, pl.ds(0, 1)], sem).wait()
        return c

    lax.fori_loop(0, tm, drain, 0, unroll=ROW_DMA_UNROLL)

    gates = gate_ref[...]
    y = x_ref[...] + gates[:, 0:1] * buf_ref[0] + gates[:, 1:2] * buf_ref[1]
    o_ref[...] = _rms(y, g_ref[...])


def _combine(dest3, x, gates, g, ys):
    n, d = x.shape
    tm = TM_ROW
    return pl.pallas_call(
        _combine_body,
        grid=(n // tm,),
        in_specs=[pl.BlockSpec((None, 1, TOP_K * tm), lambda i: (i, 0, 0), memory_space=pltpu.SMEM),
                  pl.BlockSpec((tm, d), lambda i: (i, 0)),
                  pl.BlockSpec((tm, TOP_K), lambda i: (i, 0)),
                  _const_spec(g.shape),
                  pl.BlockSpec(memory_space=pl.ANY)],
        out_specs=pl.BlockSpec((tm, d), lambda i: (i, 0)),
        out_shape=jax.ShapeDtypeStruct((n, d), jnp.float32),
        scratch_shapes=[pltpu.VMEM((TOP_K, tm, d), jnp.float32), pltpu.SemaphoreType.DMA],
        compiler_params=_cparams(("arbitrary",)),
        name="combine",
    )(dest3, x, gates, g, ys)


def _row(v):
    return v.reshape(1, -1).astype(jnp.float32)


def kernel(x, positions, ev_norm_mix, ev_w_in, ev_w_pool, ev_pool_scale, ev_ln_g, ev_ln_b, ev_w_spatial, ev_b_spatial, ev_w_out, ev_norm_ffn, ev_w_gate, ev_w_up, ev_w_down, od_norm_attn, od_w_qkv, od_lam_q1, od_lam_k1, od_lam_q2, od_lam_k2, od_subln_g, od_w_o, od_norm_moe, od_w_router, od_we_gate, od_we_up, od_we_down, final_norm):
    assert ev_norm_mix.shape[0] == 1 and od_norm_attn.shape[0] == 1, "one even and one odd layer"
    batch, seq, d = x.shape
    n = batch * seq
    bf16 = jnp.bfloat16
    gm_w = ev_ln_g.shape[-1]

    b_sp_full = jnp.repeat(ev_b_spatial[0].T, gm_w // GMLP_GROUPS, axis=1)
    x1, h1 = _mixer(x.reshape(n, d), seq, _row(ev_norm_mix[0]), ev_w_in[0].astype(bf16), ev_w_pool[0].astype(bf16),
                    _row(ev_pool_scale[0]), _row(ev_ln_g[0]), _row(ev_ln_b[0]), ev_w_spatial[0], b_sp_full,
                    ev_w_out[0].astype(bf16), _row(ev_norm_ffn[0]))
    x2, h2 = _dense_ffn(x1, h1, ev_w_gate.astype(bf16), ev_w_up.astype(bf16), ev_w_down.astype(bf16),
                        _row(od_norm_attn[0]))

    lam_init = 0.8 - 0.6 * math.exp(-0.3 * 1)
    lam = (jnp.exp(jnp.sum(od_lam_q1[0] * od_lam_k1[0])) - jnp.exp(jnp.sum(od_lam_q2[0] * od_lam_k2[0]))
           + lam_init).reshape(1).astype(jnp.float32)
    half = DIFF_HEAD_DIM // 2
    inv_freq = ROPE_THETA ** (-jnp.arange(0, DIFF_HEAD_DIM, 2, dtype=jnp.float32) / DIFF_HEAD_DIM)
    invf = jnp.tile(inv_freq, LANES // half).reshape(1, LANES)
    sign = jnp.tile(jnp.concatenate([-jnp.ones(half), jnp.ones(half)]), LANES // DIFF_HEAD_DIM)
    sign = sign.reshape(1, LANES).astype(jnp.float32)
    w_qkv = od_w_qkv[0].astype(bf16)
    pos = positions.reshape(n, 1)
    q = _proj_rope(h2, w_qkv[:, :d], pos, invf, sign, batch, DIFF_HEAD_DIM ** -0.5 * LOG2_E)
    k = _proj_rope(h2, w_qkv[:, d:2 * d], pos, invf, sign, batch, 1.0)
    vt = _proj_vt(h2, w_qkv[:, 2 * d:], batch)
    o = _attention(lam, q, k, vt, od_subln_g[0].reshape(-1, 1).astype(jnp.float32), lam_init)

    w_rt = jnp.zeros((EXPERT_ROWS, d), jnp.float32).at[:N_EXPERTS].set(od_w_router[0].T)
    x3 = _out_proj(o, x2, od_w_o[0].astype(bf16))
    route, counts = _router(x3, _row(od_norm_moe[0]), w_rt)

    tm_e = TM_MOE
    rows = (n * TOP_K // tm_e + N_EXPERTS) * tm_e
    cnt = counts[:N_EXPERTS, 0].astype(jnp.int32)
    padded = (cnt + tm_e - 1) // tm_e * tm_e
    ends = jnp.cumsum(padded)
    offs = ends - padded
    idx = route[ROUTE_IDX:ROUTE_IDX + TOP_K].astype(jnp.int32)
    rank = route[ROUTE_RANK:ROUTE_RANK + TOP_K].astype(jnp.int32)
    expert_ids = jnp.arange(N_EXPERTS, dtype=jnp.int32).reshape(N_EXPERTS, 1, 1)
    dest = rank + jnp.sum(jnp.where(idx[None] == expert_ids, offs.reshape(N_EXPERTS, 1, 1), 0), axis=0)
    gates = route[ROUTE_GATE:ROUTE_GATE + TOP_K].T
    tm_r = TM_ROW
    dest3 = dest.reshape(TOP_K, n // tm_r, tm_r).transpose(1, 0, 2).reshape(n // tm_r, 1, TOP_K * tm_r)
    n_valid = (ends[-1] // tm_e).reshape(1).astype(jnp.int32)
    tile_start = jnp.arange(rows // tm_e, dtype=jnp.int32) * tm_e
    tile_expert = jnp.minimum(jnp.sum(tile_start[:, None] >= ends[None, :], axis=1), N_EXPERTS - 1).astype(jnp.int32)

    xs = _dispatch(dest3, x3, rows)
    ys = _moe_ffn(tile_expert, n_valid, xs, _row(od_norm_moe[0]), od_we_gate[0].astype(bf16),
                  od_we_up[0].astype(bf16), od_we_down[0].astype(bf16))
    out = _combine(dest3, x3, gates, _row(final_norm), ys)
    return out.reshape(batch, seq, d)
```

```python
import functools
import math

import jax
import jax.numpy as jnp
from jax import lax
from jax.experimental import pallas as pl
from jax.experimental.pallas import tpu as pltpu

EPS = 1e-5
POOL_WINDOWS = (2, 4, 8, 16)
POOL_HALO = 16
GMLP_CHUNK = 128
GMLP_GROUPS = 8
DIFF_HEAD_DIM = 64
HEAD_WIDTH = 2 * DIFF_HEAD_DIM
ROPE_THETA = 10000.0
N_EXPERTS = 8
TOP_K = 2
LANES = 128
NEG_BIG = -1e30
LOG2_E = math.log2(math.e)

VMEM_LIMIT_BYTES = 60 * 1024 * 1024

TM_MIX = 512
MIX_SPLIT = 2
TM_FFN = 512
TF_FFN = 512
TM_PROJ = 512
PROJ_SPLIT = 4
TQ_ATT = 512
TK_ATT = 512
ATT_UNROLLS = (8, 4, 2)
TM_OUT = 512
TM_ROUTER = 1024
TM_MOE = 512
TM_ROW = 512
ROW_DMA_UNROLL = 8


def _cparams(sem):
    return pltpu.CompilerParams(dimension_semantics=sem, vmem_limit_bytes=VMEM_LIMIT_BYTES)


def _const_spec(shape):
    nd = len(shape)
    return pl.BlockSpec(shape, lambda *_: (0,) * nd, pipeline_mode=pl.Buffered(1))


def _rms(x, g):
    return x * lax.rsqrt(jnp.mean(x * x, axis=-1, keepdims=True) + EPS) * g


def _bdot(a, b):
    return jnp.dot(a, b, preferred_element_type=jnp.float32)


def _mixer_body(tiles_per_seq, x_ref, g_ref, w_in_ref, w_pool_ref, pscale_ref, ln_g_ref, ln_b_ref,
                w_sp_ref, b_sp_ref, w_out_ref, g_next_ref, x_out_ref, h_out_ref, ext_ref):
    tm = x_ref.shape[0]
    rows = tm // MIX_SPLIT
    pool_w = w_pool_ref.shape[0] * w_pool_ref.shape[1]
    pool_g = w_pool_ref.shape[1]
    gm_w = ln_g_ref.shape[1]
    gd = gm_w // GMLP_GROUPS
    i = pl.program_id(0)
    seq_tile = i % tiles_per_seq

    @pl.when(seq_tile == 0)
    def _():
        ext_ref[0:POOL_HALO, :] = jnp.zeros((POOL_HALO, pool_w), jnp.float32)

    r_i = lax.broadcasted_iota(jnp.int32, (GMLP_CHUNK, GMLP_CHUNK), 0)
    c_i = lax.broadcasted_iota(jnp.int32, (GMLP_CHUNK, GMLP_CHUNK), 1)
    tril = c_i <= r_i
    w_sp = [jnp.where(tril, w_sp_ref[g], 0.0).astype(jnp.bfloat16) for g in range(GMLP_GROUPS)]
    b_sp = b_sp_ref[...]

    for part in range(MIX_SPLIT):
        rs = slice(part * rows, (part + 1) * rows)
        x = x_ref[rs, :]
        h = _rms(x, g_ref[...]).astype(jnp.bfloat16)
        z = _bdot(h, w_in_ref[...])

        zp = z[:, :pool_w]
        ext_ref[POOL_HALO:, :] = zp
        tpos = seq_tile * tm + part * rows + lax.broadcasted_iota(jnp.int32, (rows, 1), 0)
        pooled_parts = []
        for gi, w in enumerate(POOL_WINDOWS):
            cols = slice(gi * pool_g, (gi + 1) * pool_g)
            p = ext_ref[:, cols]
            k = 1
            while k < w:
                p = p + pltpu.roll(p, k, axis=0)
                k *= 2
            cnt = jnp.minimum(tpos + 1, w).astype(jnp.float32)
            pooled = p[POOL_HALO:, :] / cnt - zp[:, cols]
            pooled_parts.append(_bdot(pooled.astype(jnp.bfloat16), w_pool_ref[gi]))
        ext_ref[0:POOL_HALO, :] = zp[rows - POOL_HALO:, :]
        y_pool = jnp.concatenate(pooled_parts, axis=-1) * pscale_ref[...]

        a = z[:, pool_w:]
        a = 0.5 * a * (1.0 + lax.erf(a * (1.0 / math.sqrt(2.0))))
        u = a[:, :gm_w]
        v = a[:, gm_w:]
        mu = jnp.mean(v, axis=-1, keepdims=True)
        vc = v - mu
        var = jnp.mean(vc * vc, axis=-1, keepdims=True)
        vn = (vc * lax.rsqrt(var + EPS) * ln_g_ref[...] + ln_b_ref[...]).astype(jnp.bfloat16)
        rows_out = []
        for c in range(rows // GMLP_CHUNK):
            cs = slice(c * GMLP_CHUNK, (c + 1) * GMLP_CHUNK)
            mixed = jnp.concatenate(
                [_bdot(w_sp[g], vn[cs, g * gd:(g + 1) * gd]) for g in range(GMLP_GROUPS)], axis=-1)
            rows_out.append(u[cs, :] * (mixed + b_sp))
        y_gate = jnp.concatenate(rows_out, axis=0)

        y = jnp.concatenate([y_pool, y_gate], axis=-1).astype(jnp.bfloat16)
        x1 = x + _bdot(y, w_out_ref[...])
        x_out_ref[rs, :] = x1
        h_out_ref[rs, :] = _rms(x1, g_next_ref[...]).astype(jnp.bfloat16)


def _mixer(x, seq, g, w_in, w_pool, pscale, ln_g, ln_b, w_sp, b_sp_full, w_out, g_next):
    n, d = x.shape
    tm = TM_MIX
    rows = tm // MIX_SPLIT
    assert seq % tm == 0 and rows % GMLP_CHUNK == 0 and rows >= POOL_HALO
    pool_w = w_pool.shape[0] * w_pool.shape[1]
    row_spec = pl.BlockSpec((tm, d), lambda i: (i, 0))
    return pl.pallas_call(
        functools.partial(_mixer_body, seq // tm),
        grid=(n // tm,),
        in_specs=[row_spec, _const_spec(g.shape), _const_spec(w_in.shape), _const_spec(w_pool.shape),
                  _const_spec(pscale.shape), _const_spec(ln_g.shape), _const_spec(ln_b.shape),
                  _const_spec(w_sp.shape), _const_spec(b_sp_full.shape), _const_spec(w_out.shape),
                  _const_spec(g_next.shape)],
        out_specs=[row_spec, row_spec],
        out_shape=[jax.ShapeDtypeStruct((n, d), jnp.float32), jax.ShapeDtypeStruct((n, d), jnp.bfloat16)],
        scratch_shapes=[pltpu.VMEM((rows + POOL_HALO, pool_w), jnp.float32)],
        compiler_params=_cparams(("arbitrary",)),
        name="mixer0",
    )(x, g, w_in, w_pool, pscale, ln_g, ln_b, w_sp, b_sp_full, w_out, g_next)


def _swiglu_step(h, wg_ref, wu_ref, wd_ref):
    a = _bdot(h, wg_ref[...])
    b = _bdot(h, wu_ref[...])
    y = (a * (1.0 / (1.0 + jnp.exp(-a))) * b).astype(jnp.bfloat16)
    return _bdot(y, wd_ref[...])


def _dense_ffn_body(te_ref, x_ref, h_ref, wg_ref, wu_ref, wd_ref, g_next_ref, x_out_ref, h_out_ref, acc_ref):
    j = pl.program_id(1)

    @pl.when(j == 0)
    def _():
        acc_ref[...] = _swiglu_step(h_ref[...], wg_ref, wu_ref, wd_ref)

    @pl.when(j > 0)
    def _():
        acc_ref[...] += _swiglu_step(h_ref[...], wg_ref, wu_ref, wd_ref)

    @pl.when(j == pl.num_programs(1) - 1)
    def _():
        x2 = x_ref[...] + acc_ref[...]
        x_out_ref[...] = x2
        h_out_ref[...] = _rms(x2, g_next_ref[...]).astype(jnp.bfloat16)


def _moe_ffn_body(te_ref, nv_ref, x_ref, g_ref, wg_ref, wu_ref, wd_ref, y_out_ref, h_ref, acc_ref):
    i = pl.program_id(0)
    j = pl.program_id(1)

    @pl.when(i < nv_ref[0])
    def _():
        @pl.when(j == 0)
        def _():
            h = _rms(x_ref[...], g_ref[...]).astype(jnp.bfloat16)
            h_ref[...] = h
            acc_ref[...] = _swiglu_step(h, wg_ref, wu_ref, wd_ref)

        @pl.when(j > 0)
        def _():
            acc_ref[...] += _swiglu_step(h_ref[...], wg_ref, wu_ref, wd_ref)

        @pl.when(j == pl.num_programs(1) - 1)
        def _():
            y_out_ref[...] = acc_ref[...]

    @pl.when((i >= nv_ref[0]) & (j == 0))
    def _():
        y_out_ref[...] = jnp.zeros_like(y_out_ref)


def _weight_specs(d, ff, tf, expert_of):
    wg_spec = pl.BlockSpec((None, d, tf), lambda i, j, *s: (expert_of(i, *s), 0, j))
    wd_spec = pl.BlockSpec((None, tf, d), lambda i, j, *s: (expert_of(i, *s), j, 0))
    return wg_spec, wg_spec, wd_spec


def _dense_ffn(x, h, wg, wu, wd, g_next):
    n, d = x.shape
    ff = wg.shape[-1]
    tm, tf = TM_FFN, TF_FFN
    assert n % tm == 0 and ff % tf == 0
    row_spec = pl.BlockSpec((tm, d), lambda i, j, *s: (i, 0))
    zero = jnp.zeros((1,), jnp.int32)
    return pl.pallas_call(
        _dense_ffn_body,
        grid_spec=pltpu.PrefetchScalarGridSpec(
            num_scalar_prefetch=1,
            grid=(n // tm, ff // tf),
            in_specs=[row_spec, row_spec, *_weight_specs(d, ff, tf, lambda i, te: te[0]),
                      pl.BlockSpec(g_next.shape, lambda i, j, *s: (0, 0))],
            out_specs=[row_spec, row_spec],
            scratch_shapes=[pltpu.VMEM((tm, d), jnp.float32)],
        ),
        out_shape=[jax.ShapeDtypeStruct((n, d), jnp.float32), jax.ShapeDtypeStruct((n, d), jnp.bfloat16)],
        compiler_params=_cparams(("arbitrary", "arbitrary")),
        name="dense_ffn",
    )(zero, x, h, wg, wu, wd, g_next)


def _moe_ffn(tile_expert, n_valid, xs, g, wg, wu, wd):
    r, d = xs.shape
    ff = wg.shape[-1]
    tm, tf = TM_MOE, TF_FFN
    assert r % tm == 0 and ff % tf == 0
    n_tiles = r // tm

    def row_idx(i, j, te, nv):
        return (jnp.minimum(i, nv[0] - 1), 0)

    def expert_of(i, te, nv):
        return te[jnp.minimum(i, nv[0] - 1)]

    def ff_idx(i, j, nv):
        return jnp.where(i < nv[0], j, ff // tf - 1)

    wg_spec = pl.BlockSpec((None, d, tf), lambda i, j, te, nv: (expert_of(i, te, nv), 0, ff_idx(i, j, nv)))
    wd_spec = pl.BlockSpec((None, tf, d), lambda i, j, te, nv: (expert_of(i, te, nv), ff_idx(i, j, nv), 0))
    row_spec = pl.BlockSpec((tm, d), row_idx)
    return pl.pallas_call(
        _moe_ffn_body,
        grid_spec=pltpu.PrefetchScalarGridSpec(
            num_scalar_prefetch=2,
            grid=(n_tiles, ff // tf),
            in_specs=[row_spec, pl.BlockSpec(g.shape, lambda i, j, *s: (0, 0)), wg_spec, wg_spec, wd_spec],
            out_specs=pl.BlockSpec((tm, d), lambda i, j, *s: (i, 0)),
            scratch_shapes=[pltpu.VMEM((tm, d), jnp.bfloat16), pltpu.VMEM((tm, d), jnp.float32)],
        ),
        out_shape=jax.ShapeDtypeStruct((r, d), jnp.float32),
        compiler_params=_cparams(("arbitrary", "arbitrary")),
        name="moe_ffn",
    )(tile_expert, n_valid, xs, g, wg, wu, wd)


def _proj_rope_body(scale, h_ref, w_ref, pos_ref, invf_ref, sign_ref, o_ref):
    half = DIFF_HEAD_DIM // 2
    ang = pos_ref[...].astype(jnp.float32) * invf_ref[...]
    cos = jnp.cos(ang) * scale
    sin = jnp.sin(ang) * sign_ref[...] * scale
    lane = lax.broadcasted_iota(jnp.int32, (1, LANES), 1)
    first_half = (lane % DIFF_HEAD_DIM) < half
    rows = h_ref.shape[0] // PROJ_SPLIT
    for part in range(PROJ_SPLIT):
        rs = slice(part * rows, (part + 1) * rows)
        t = _bdot(h_ref[rs, :], w_ref[...])
        for hd in range(o_ref.shape[0]):
            blk = t[:, hd * HEAD_WIDTH:(hd + 1) * HEAD_WIDTH]
            partner = jnp.where(first_half, pltpu.roll(blk, LANES - half, axis=1), pltpu.roll(blk, half, axis=1))
            o_ref[hd, rs, :] = (blk * cos[rs] + partner * sin[rs]).astype(o_ref.dtype)


def _proj_rope(h, w, pos, invf, sign, batch, scale):
    n, d = h.shape
    seq = n // batch
    tm = TM_PROJ
    heads = d // HEAD_WIDTH
    tiles = seq // tm
    return pl.pallas_call(
        functools.partial(_proj_rope_body, scale),
        grid=(n // tm,),
        in_specs=[pl.BlockSpec((tm, d), lambda i: (i, 0)), _const_spec(w.shape),
                  pl.BlockSpec((tm, 1), lambda i: (i, 0)), _const_spec(invf.shape), _const_spec(sign.shape)],
        out_specs=pl.BlockSpec((None, heads, tm, HEAD_WIDTH), lambda i: (i // tiles, 0, i % tiles, 0)),
        out_shape=jax.ShapeDtypeStruct((batch, heads, seq, HEAD_WIDTH), jnp.bfloat16),
        compiler_params=_cparams(("parallel",)),
        name="proj_rope",
    )(h, w, pos, invf, sign)


def _proj_vt_body(h_ref, w_ref, o_ref):
    t = _bdot(h_ref[...], w_ref[...])
    for hd in range(o_ref.shape[0]):
        o_ref[hd, 0] = t[:, hd * HEAD_WIDTH:(hd + 1) * HEAD_WIDTH].T.astype(o_ref.dtype)


def _proj_vt(h, w, batch):
    n, d = h.shape
    seq = n // batch
    tm = TK_ATT
    heads = d // HEAD_WIDTH
    tiles = seq // tm
    return pl.pallas_call(
        _proj_vt_body,
        grid=(n // tm,),
        in_specs=[pl.BlockSpec((tm, d), lambda i: (i, 0)), _const_spec(w.shape)],
        out_specs=pl.BlockSpec((None, heads, 1, HEAD_WIDTH, tm), lambda i: (i // tiles, 0, i % tiles, 0, 0)),
        out_shape=jax.ShapeDtypeStruct((batch, heads, tiles, HEAD_WIDTH, tm), jnp.bfloat16),
        compiler_params=_cparams(("parallel",)),
        name="proj_vt",
    )(h, w)


def _attn_body(lam_init, lam_ref, q_ref, k_ref, vt_ref, g_ref, o_ref, s_ref, cmax_ref, m_ref, l_ref, acc_ref, bias_ref):
    tq = q_ref.shape[0]
    tk = vt_ref.shape[2]
    qi = pl.program_id(2)
    lane = lax.broadcasted_iota(jnp.int32, (1, HEAD_WIDTH), 1)
    q = q_ref[...]
    zero = jnp.zeros_like(q)
    qz = (jnp.where(lane < DIFF_HEAD_DIM, q, zero), jnp.where(lane >= DIFF_HEAD_DIM, q, zero))

    m_ref[...] = jnp.full_like(m_ref, NEG_BIG)
    l_ref[...] = jnp.zeros_like(l_ref)
    acc_ref[...] = jnp.zeros_like(acc_ref)

    @pl.when(qi == 0)
    def _():
        krow = lax.broadcasted_iota(jnp.int32, (tk, tq), 0)
        qcol = lax.broadcasted_iota(jnp.int32, (tk, tq), 1)
        bias_ref[...] = jnp.where(krow <= qcol, 0.0, NEG_BIG)

    def scores(kv, slot, masked, c):
        start = pl.multiple_of(kv * tk, tk)
        kb = k_ref[pl.ds(start, tk), :]
        s = lax.dot_general(kb, qz[c], (((1,), (1,)), ((), ())), preferred_element_type=jnp.float32)
        if masked:
            s = s + bias_ref[...]
        s_ref[slot, c] = s
        cmax_ref[slot, c] = jnp.max(s, axis=0, keepdims=True)

    def consume(kv, slot, c):
        m_old = m_ref[c]
        m_new = jnp.maximum(m_old, cmax_ref[slot, c])
        alpha = jnp.exp2(m_old - m_new)
        p = jnp.exp2(s_ref[slot, c] - m_new)
        l_ref[c] = alpha * l_ref[c] + jnp.sum(p, axis=0, keepdims=True)
        acc_ref[c] = alpha * acc_ref[c] + _bdot(vt_ref[kv], p.astype(jnp.bfloat16))
        m_ref[c] = m_new

    for c in range(2):
        scores(qi, 0, True, c)

    def step(j, read_slot):
        kv_cur = jnp.where(j == 0, qi, j - 1)
        for c in range(2):
            scores(j, 1 - read_slot, False, c)
            consume(kv_cur, read_slot, c)

    done = 0
    for group in ATT_UNROLLS:
        def run(t, carry, group=group, done=done):
            for u in range(group):
                step(done + group * t + u, u % 2)
            return carry

        trips = (qi - done) // group
        lax.fori_loop(0, trips, run, 0)
        done = done + trips * group
    last = jnp.where(qi == 0, qi, qi - 1)

    @pl.when(qi % 2 == 1)
    def _():
        step(qi - 1, 0)
        for c in range(2):
            consume(last, 1, c)

    @pl.when(qi % 2 == 0)
    def _():
        for c in range(2):
            consume(last, 0, c)

    o = acc_ref[0] / l_ref[0] - lam_ref[0] * (acc_ref[1] / l_ref[1])
    o = o * lax.rsqrt(jnp.mean(o * o, axis=0, keepdims=True) + EPS) * g_ref[...] * (1.0 - lam_init)
    o_ref[...] = o.T.astype(o_ref.dtype)


def _attention(lam, q, k, vt, g_col, lam_init):
    batch, heads, seq, hw = q.shape
    tq, tk = TQ_ATT, TK_ATT
    assert tq == tk and seq % tq == 0
    return pl.pallas_call(
        functools.partial(_attn_body, lam_init),
        grid=(batch, heads, seq // tq),
        in_specs=[pl.BlockSpec(memory_space=pltpu.SMEM),
                  pl.BlockSpec((None, None, tq, hw), lambda b, h, i: (b, h, i, 0)),
                  pl.BlockSpec((None, None, seq, hw), lambda b, h, i: (b, h, 0, 0)),
                  pl.BlockSpec((None, None, seq // tk, hw, tk), lambda b, h, i: (b, h, 0, 0, 0)),
                  pl.BlockSpec(g_col.shape, lambda b, h, i: (0, 0))],
        out_specs=pl.BlockSpec((None, None, tq, hw), lambda b, h, i: (b, h, i, 0)),
        out_shape=jax.ShapeDtypeStruct((batch, heads, seq, hw), jnp.bfloat16),
        scratch_shapes=[pltpu.VMEM((2, 2, tk, tq), jnp.float32), pltpu.VMEM((2, 2, 1, tq), jnp.float32),
                        pltpu.VMEM((2, 1, tq), jnp.float32), pltpu.VMEM((2, 1, tq), jnp.float32),
                        pltpu.VMEM((2, hw, tq), jnp.float32), pltpu.VMEM((tk, tq), jnp.float32)],
        compiler_params=_cparams(("parallel", "parallel", "arbitrary")),
        name="diff_attn",
    )(lam, q, k, vt, g_col)


ROUTE_IDX, ROUTE_GATE, ROUTE_RANK = 0, 2, 4
ROUTE_ROWS = 8
EXPERT_ROWS = 16


def _out_proj_body(o_ref, x_ref, w_o_ref, x_out_ref):
    o = jnp.concatenate([o_ref[hd] for hd in range(o_ref.shape[0])], axis=-1)
    x_out_ref[...] = x_ref[...] + _bdot(o, w_o_ref[...])


def _out_proj(o, x, w_o):
    batch, heads, seq, hw = o.shape
    d = heads * hw
    tm = TM_OUT
    tiles = seq // tm
    row_spec = pl.BlockSpec((tm, d), lambda b, i: (b * tiles + i, 0))
    return pl.pallas_call(
        _out_proj_body,
        grid=(batch, tiles),
        in_specs=[pl.BlockSpec((None, heads, tm, hw), lambda b, i: (b, 0, i, 0)), row_spec, _const_spec(w_o.shape)],
        out_specs=row_spec,
        out_shape=jax.ShapeDtypeStruct(x.shape, jnp.float32),
        compiler_params=_cparams(("parallel", "parallel")),
        name="out_proj",
    )(o, x, w_o)


def _router_body(x_ref, g_ref, w_rt_ref, route_ref, count_ref, run_ref):
    tm = x_ref.shape[0]

    @pl.when(pl.program_id(0) == 0)
    def _():
        run_ref[...] = jnp.zeros_like(run_ref)

    h = _rms(x_ref[...], g_ref[...])
    h_hi = h.astype(jnp.bfloat16)
    h_lo = (h - h_hi.astype(jnp.float32)).astype(jnp.bfloat16)
    w_rt = w_rt_ref[...]
    w_hi = w_rt.astype(jnp.bfloat16)
    w_lo = (w_rt - w_hi.astype(jnp.float32)).astype(jnp.bfloat16)
    nt = (((1,), (1,)), ((), ()))
    logits = (lax.dot_general(w_hi, h_hi, nt, preferred_element_type=jnp.float32)
              + lax.dot_general(w_lo, h_hi, nt, preferred_element_type=jnp.float32)
              + lax.dot_general(w_hi, h_lo, nt, preferred_element_type=jnp.float32))
    row = lax.broadcasted_iota(jnp.int32, (EXPERT_ROWS, tm), 0).astype(jnp.float32)
    logits = jnp.where(row < N_EXPERTS, logits, -jnp.inf)
    m1 = jnp.max(logits, axis=0, keepdims=True)
    e1 = jnp.min(jnp.where(logits == m1, row, float(EXPERT_ROWS)), axis=0, keepdims=True)
    sel1 = row == e1
    rest = jnp.where(sel1, -jnp.inf, logits)
    m2 = jnp.max(rest, axis=0, keepdims=True)
    e2 = jnp.min(jnp.where(rest == m2, row, float(EXPERT_ROWS)), axis=0, keepdims=True)
    sel2 = row == e2
    t = jnp.exp(m2 - m1)
    w1 = 1.0 / (1.0 + t)
    w2 = t / (1.0 + t)

    onehot = (sel1 | sel2).astype(jnp.bfloat16)
    r_i = lax.broadcasted_iota(jnp.int32, (tm, tm), 0)
    c_i = lax.broadcasted_iota(jnp.int32, (tm, tm), 1)
    earlier = (r_i < c_i).astype(jnp.bfloat16)
    before = _bdot(onehot, earlier) + run_ref[:, 0:1]
    run_ref[...] += jnp.sum(onehot.astype(jnp.float32), axis=1, keepdims=True)
    rank1 = jnp.sum(jnp.where(sel1, before, 0.0), axis=0, keepdims=True)
    rank2 = jnp.sum(jnp.where(sel2, before, 0.0), axis=0, keepdims=True)

    rrow = lax.broadcasted_iota(jnp.int32, (ROUTE_ROWS, tm), 0)
    route = jnp.zeros((ROUTE_ROWS, tm), jnp.float32)
    for rw, val in ((ROUTE_IDX, e1), (ROUTE_IDX + 1, e2), (ROUTE_GATE, w1), (ROUTE_GATE + 1, w2),
                    (ROUTE_RANK, rank1), (ROUTE_RANK + 1, rank2)):
        route = jnp.where(rrow == rw, val, route)
    route_ref[...] = route
    count_ref[...] = run_ref[...]


def _router(x, g, w_rt):
    n, d = x.shape
    tm = TM_ROUTER
    return pl.pallas_call(
        _router_body,
        grid=(n // tm,),
        in_specs=[pl.BlockSpec((tm, d), lambda i: (i, 0)), _const_spec(g.shape), _const_spec(w_rt.shape)],
        out_specs=[pl.BlockSpec((ROUTE_ROWS, tm), lambda i: (0, i)),
                   pl.BlockSpec((EXPERT_ROWS, LANES), lambda i: (0, 0))],
        out_shape=[jax.ShapeDtypeStruct((ROUTE_ROWS, n), jnp.float32),
                   jax.ShapeDtypeStruct((EXPERT_ROWS, LANES), jnp.float32)],
        scratch_shapes=[pltpu.VMEM((EXPERT_ROWS, LANES), jnp.float32)],
        compiler_params=_cparams(("arbitrary",)),
        name="router",
    )(x, g, w_rt)


def _row_copy(src_ref, dst_ref, sem):
    return pltpu.make_async_copy(src_ref, dst_ref, sem)


def _dispatch_body(dest_ref, x_ref, xs_in_ref, xs_ref, sem):
    del xs_in_ref
    tm = x_ref.shape[0]

    def issue(r, c):
        for k in range(TOP_K):
            _row_copy(x_ref.at[pl.ds(r, 1)], xs_ref.at[pl.ds(dest_ref[0, k * tm + r], 1)], sem).start(priority=k)
        return c

    lax.fori_loop(0, tm, issue, 0, unroll=ROW_DMA_UNROLL)

    def drain(r, c):
        for k in range(TOP_K):
            _row_copy(x_ref.at[pl.ds(0, 1)], xs_ref.at[pl.ds(0, 1)], sem).wait()
        return c

    lax.fori_loop(0, tm, drain, 0, unroll=ROW_DMA_UNROLL)


def _dispatch(dest3, x, rows):
    n, d = x.shape
    tm = TM_ROW
    zeros = jnp.zeros((rows, d), x.dtype)
    return pl.pallas_call(
        _dispatch_body,
        grid=(n // tm,),
        in_specs=[pl.BlockSpec((None, 1, TOP_K * tm), lambda i: (i, 0, 0), memory_space=pltpu.SMEM),
                  pl.BlockSpec((tm, d), lambda i: (i, 0)),
                  pl.BlockSpec(memory_space=pl.ANY)],
        out_specs=pl.BlockSpec(memory_space=pl.ANY),
        out_shape=jax.ShapeDtypeStruct((rows, d), x.dtype),
        scratch_shapes=[pltpu.SemaphoreType.DMA],
        input_output_aliases={2: 0},
        compiler_params=_cparams(("arbitrary",)),
        name="dispatch",
    )(dest3, x, zeros)


def _combine_body(dest_ref, x_ref, gate_ref, g_ref, ys_ref, o_ref, buf_ref, sem):
    tm = x_ref.shape[0]

    def issue(r, c):
        for k in range(TOP_K):
            _row_copy(ys_ref.at[pl.ds(dest_ref[0, k * tm + r], 1)], buf_ref.at[k, pl.ds(r, 1)], sem).start(priority=k)
        return c

    lax.fori_loop(0, tm, issue, 0, unroll=ROW_DMA_UNROLL)

    def drain(r, c):
        for k in range(TOP_K):
            _row_copy(ys_ref.at[pl.ds(0, 1)], buf_ref.at[k, pl.ds(0, 1)], sem).wait()
        return c

    lax.fori_loop(0, tm, drain, 0, unroll=ROW_DMA_UNROLL)

    gates = gate_ref[...]
    y = x_ref[...] + gates[:, 0:1] * buf_ref[0] + gates[:, 1:2] * buf_ref[1]
    o_ref[...] = _rms(y, g_ref[...])


def _combine(dest3, x, gates, g, ys):
    n, d = x.shape
    tm = TM_ROW
    return pl.pallas_call(
        _combine_body,
        grid=(n // tm,),
        in_specs=[pl.BlockSpec((None, 1, TOP_K * tm), lambda i: (i, 0, 0), memory_space=pltpu.SMEM),
                  pl.BlockSpec((tm, d), lambda i: (i, 0)),
                  pl.BlockSpec((tm, TOP_K), lambda i: (i, 0)),
                  _const_spec(g.shape),
                  pl.BlockSpec(memory_space=pl.ANY)],
        out_specs=pl.BlockSpec((tm, d), lambda i: (i, 0)),
        out_shape=jax.ShapeDtypeStruct((n, d), jnp.float32),
        scratch_shapes=[pltpu.VMEM((TOP_K, tm, d), jnp.float32), pltpu.SemaphoreType.DMA],
        compiler_params=_cparams(("arbitrary",)),
        name="combine",
    )(dest3, x, gates, g, ys)


def _row(v):
    return v.reshape(1, -1).astype(jnp.float32)


def kernel(x, positions, ev_norm_mix, ev_w_in, ev_w_pool, ev_pool_scale, ev_ln_g, ev_ln_b, ev_w_spatial, ev_b_spatial, ev_w_out, ev_norm_ffn, ev_w_gate, ev_w_up, ev_w_down, od_norm_attn, od_w_qkv, od_lam_q1, od_lam_k1, od_lam_q2, od_lam_k2, od_subln_g, od_w_o, od_norm_moe, od_w_router, od_we_gate, od_we_up, od_we_down, final_norm):
    assert ev_norm_mix.shape[0] == 1 and od_norm_attn.shape[0] == 1, "one even and one odd layer"
    batch, seq, d = x.shape
    n = batch * seq
    bf16 = jnp.bfloat16
    gm_w = ev_ln_g.shape[-1]

    b_sp_full = jnp.repeat(ev_b_spatial[0].T, gm_w // GMLP_GROUPS, axis=1)
    x1, h1 = _mixer(x.reshape(n, d), seq, _row(ev_norm_mix[0]), ev_w_in[0].astype(bf16), ev_w_pool[0].astype(bf16),
                    _row(ev_pool_scale[0]), _row(ev_ln_g[0]), _row(ev_ln_b[0]), ev_w_spatial[0], b_sp_full,
                    ev_w_out[0].astype(bf16), _row(ev_norm_ffn[0]))
    x2, h2 = _dense_ffn(x1, h1, ev_w_gate.astype(bf16), ev_w_up.astype(bf16), ev_w_down.astype(bf16),
                        _row(od_norm_attn[0]))

    lam_init = 0.8 - 0.6 * math.exp(-0.3 * 1)
    lam = (jnp.exp(jnp.sum(od_lam_q1[0] * od_lam_k1[0])) - jnp.exp(jnp.sum(od_lam_q2[0] * od_lam_k2[0]))
           + lam_init).reshape(1).astype(jnp.float32)
    half = DIFF_HEAD_DIM // 2
    inv_freq = ROPE_THETA ** (-jnp.arange(0, DIFF_HEAD_DIM, 2, dtype=jnp.float32) / DIFF_HEAD_DIM)
    invf = jnp.tile(inv_freq, LANES // half).reshape(1, LANES)
    sign = jnp.tile(jnp.concatenate([-jnp.ones(half), jnp.ones(half)]), LANES // DIFF_HEAD_DIM)
    sign = sign.reshape(1, LANES).astype(jnp.float32)
    w_qkv = od_w_qkv[0].astype(bf16)
    pos = positions.reshape(n, 1)
    q = _proj_rope(h2, w_qkv[:, :d], pos, invf, sign, batch, DIFF_HEAD_DIM ** -0.5 * LOG2_E)
    k = _proj_rope(h2, w_qkv[:, d:2 * d], pos, invf, sign, batch, 1.0)
    vt = _proj_vt(h2, w_qkv[:, 2 * d:], batch)
    o = _attention(lam, q, k, vt, od_subln_g[0].reshape(-1, 1).astype(jnp.float32), lam_init)

    w_rt = jnp.zeros((EXPERT_ROWS, d), jnp.float32).at[:N_EXPERTS].set(od_w_router[0].T)
    x3 = _out_proj(o, x2, od_w_o[0].astype(bf16))
    route, counts = _router(x3, _row(od_norm_moe[0]), w_rt)

    tm_e = TM_MOE
    rows = (n * TOP_K // tm_e + N_EXPERTS) * tm_e
    cnt = counts[:N_EXPERTS, 0].astype(jnp.int32)
    padded = (cnt + tm_e - 1) // tm_e * tm_e
    ends = jnp.cumsum(padded)
    offs = ends - padded
    idx = route[ROUTE_IDX:ROUTE_IDX + TOP_K].astype(jnp.int32)
    rank = route[ROUTE_RANK:ROUTE_RANK + TOP_K].astype(jnp.int32)
    expert_ids = jnp.arange(N_EXPERTS, dtype=jnp.int32).reshape(N_EXPERTS, 1, 1)
    dest = rank + jnp.sum(jnp.where(idx[None] == expert_ids, offs.reshape(N_EXPERTS, 1, 1), 0), axis=0)
    gates = route[ROUTE_GATE:ROUTE_GATE + TOP_K].T
    tm_r = TM_ROW
    dest3 = dest.reshape(TOP_K, n // tm_r, tm_r).transpose(1, 0, 2).reshape(n // tm_r, 1, TOP_K * tm_r)
    n_valid = (ends[-1] // tm_e).reshape(1).astype(jnp.int32)
    tile_start = jnp.arange(rows // tm_e, dtype=jnp.int32) * tm_e
    tile_expert = jnp.minimum(jnp.sum(tile_start[:, None] >= ends[None, :], axis=1), N_EXPERTS - 1).astype(jnp.int32)

    xs = _dispatch(dest3, x3, rows)
    ys = _moe_ffn(tile_expert, n_valid, xs, _row(od_norm_moe[0]), od_we_gate[0].astype(bf16),
                  od_we_up[0].astype(bf16), od_we_down[0].astype(bf16))
    out = _combine(dest3, x3, gates, _row(final_norm), ys)
    return out.reshape(batch, seq, d)
```

```python
import functools
import math

import jax
import jax.numpy as jnp
from jax import lax
from jax.experimental import pallas as pl
from jax.experimental.pallas import tpu as pltpu

EPS = 1e-5
POOL_WINDOWS = (2, 4, 8, 16)
POOL_HALO = 16
GMLP_CHUNK = 128
GMLP_GROUPS = 8
DIFF_HEAD_DIM = 64
HEAD_WIDTH = 2 * DIFF_HEAD_DIM
ROPE_THETA = 10000.0
N_EXPERTS = 8
TOP_K = 2
LANES = 128
NEG_BIG = -1e30
LOG2_E = math.log2(math.e)

VMEM_LIMIT_BYTES = 60 * 1024 * 1024

TM_MIX = 512
MIX_SPLIT = 2
TM_FFN = 512
TF_FFN = 512
TM_PROJ = 512
PROJ_SPLIT = 4
TQ_ATT = 512
TK_ATT = 512
ATT_UNROLLS = (16, 8, 4, 2)
TM_OUT = 512
TM_ROUTER = 1024
TM_MOE = 512
TM_ROW = 512
ROW_DMA_UNROLL = 8


def _cparams(sem):
    return pltpu.CompilerParams(dimension_semantics=sem, vmem_limit_bytes=VMEM_LIMIT_BYTES)


def _const_spec(shape):
    nd = len(shape)
    return pl.BlockSpec(shape, lambda *_: (0,) * nd, pipeline_mode=pl.Buffered(1))


def _rms(x, g):
    return x * lax.rsqrt(jnp.mean(x * x, axis=-1, keepdims=True) + EPS) * g


def _bdot(a, b):
    return jnp.dot(a, b, preferred_element_type=jnp.float32)


def _mixer_body(tiles_per_seq, x_ref, g_ref, w_in_ref, w_pool_ref, pscale_ref, ln_g_ref, ln_b_ref,
                w_sp_ref, b_sp_ref, w_out_ref, g_next_ref, x_out_ref, h_out_ref, ext_ref):
    tm = x_ref.shape[0]
    rows = tm // MIX_SPLIT
    pool_w = w_pool_ref.shape[0] * w_pool_ref.shape[1]
    pool_g = w_pool_ref.shape[1]
    gm_w = ln_g_ref.shape[1]
    gd = gm_w // GMLP_GROUPS
    i = pl.program_id(0)
    seq_tile = i % tiles_per_seq

    @pl.when(seq_tile == 0)
    def _():
        ext_ref[0:POOL_HALO, :] = jnp.zeros((POOL_HALO, pool_w), jnp.float32)

    r_i = lax.broadcasted_iota(jnp.int32, (GMLP_CHUNK, GMLP_CHUNK), 0)
    c_i = lax.broadcasted_iota(jnp.int32, (GMLP_CHUNK, GMLP_CHUNK), 1)
    tril = c_i <= r_i
    w_sp = [jnp.where(tril, w_sp_ref[g], 0.0).astype(jnp.bfloat16) for g in range(GMLP_GROUPS)]
    b_sp = b_sp_ref[...]

    for part in range(MIX_SPLIT):
        rs = slice(part * rows, (part + 1) * rows)
        x = x_ref[rs, :]
        h = _rms(x, g_ref[...]).astype(jnp.bfloat16)
        z = _bdot(h, w_in_ref[...])

        zp = z[:, :pool_w]
        ext_ref[POOL_HALO:, :] = zp
        tpos = seq_tile * tm + part * rows + lax.broadcasted_iota(jnp.int32, (rows, 1), 0)
        pooled_parts = []
        for gi, w in enumerate(POOL_WINDOWS):
            cols = slice(gi * pool_g, (gi + 1) * pool_g)
            p = ext_ref[:, cols]
            k = 1
            while k < w:
                p = p + pltpu.roll(p, k, axis=0)
                k *= 2
            cnt = jnp.minimum(tpos + 1, w).astype(jnp.float32)
            pooled = p[POOL_HALO:, :] / cnt - zp[:, cols]
            pooled_parts.append(_bdot(pooled.astype(jnp.bfloat16), w_pool_ref[gi]))
        ext_ref[0:POOL_HALO, :] = zp[rows - POOL_HALO:, :]
        y_pool = jnp.concatenate(pooled_parts, axis=-1) * pscale_ref[...]

        a = z[:, pool_w:]
        a = 0.5 * a * (1.0 + lax.erf(a * (1.0 / math.sqrt(2.0))))
        u = a[:, :gm_w]
        v = a[:, gm_w:]
        mu = jnp.mean(v, axis=-1, keepdims=True)
        vc = v - mu
        var = jnp.mean(vc * vc, axis=-1, keepdims=True)
        vn = (vc * lax.rsqrt(var + EPS) * ln_g_ref[...] + ln_b_ref[...]).astype(jnp.bfloat16)
        rows_out = []
        for c in range(rows // GMLP_CHUNK):
            cs = slice(c * GMLP_CHUNK, (c + 1) * GMLP_CHUNK)
            mixed = jnp.concatenate(
                [_bdot(w_sp[g], vn[cs, g * gd:(g + 1) * gd]) for g in range(GMLP_GROUPS)], axis=-1)
            rows_out.append(u[cs, :] * (mixed + b_sp))
        y_gate = jnp.concatenate(rows_out, axis=0)

        y = jnp.concatenate([y_pool, y_gate], axis=-1).astype(jnp.bfloat16)
        x1 = x + _bdot(y, w_out_ref[...])
        x_out_ref[rs, :] = x1
        h_out_ref[rs, :] = _rms(x1, g_next_ref[...]).astype(jnp.bfloat16)


def _mixer(x, seq, g, w_in, w_pool, pscale, ln_g, ln_b, w_sp, b_sp_full, w_out, g_next):
    n, d = x.shape
    tm = TM_MIX
    rows = tm // MIX_SPLIT
    assert seq % tm == 0 and rows % GMLP_CHUNK == 0 and rows >= POOL_HALO
    pool_w = w_pool.shape[0] * w_pool.shape[1]
    row_spec = pl.BlockSpec((tm, d), lambda i: (i, 0))
    return pl.pallas_call(
        functools.partial(_mixer_body, seq // tm),
        grid=(n // tm,),
        in_specs=[row_spec, _const_spec(g.shape), _const_spec(w_in.shape), _const_spec(w_pool.shape),
                  _const_spec(pscale.shape), _const_spec(ln_g.shape), _const_spec(ln_b.shape),
                  _const_spec(w_sp.shape), _const_spec(b_sp_full.shape), _const_spec(w_out.shape),
                  _const_spec(g_next.shape)],
        out_specs=[row_spec, row_spec],
        out_shape=[jax.ShapeDtypeStruct((n, d), jnp.float32), jax.ShapeDtypeStruct((n, d), jnp.bfloat16)],
        scratch_shapes=[pltpu.VMEM((rows + POOL_HALO, pool_w), jnp.float32)],
        compiler_params=_cparams(("arbitrary",)),
        name="mixer0",
    )(x, g, w_in, w_pool, pscale, ln_g, ln_b, w_sp, b_sp_full, w_out, g_next)


def _swiglu_step(h, wg_ref, wu_ref, wd_ref):
    a = _bdot(h, wg_ref[...])
    b = _bdot(h, wu_ref[...])
    y = (a * (1.0 / (1.0 + jnp.exp(-a))) * b).astype(jnp.bfloat16)
    return _bdot(y, wd_ref[...])


def _dense_ffn_body(te_ref, x_ref, h_ref, wg_ref, wu_ref, wd_ref, g_next_ref, x_out_ref, h_out_ref, acc_ref):
    j = pl.program_id(1)

    @pl.when(j == 0)
    def _():
        acc_ref[...] = _swiglu_step(h_ref[...], wg_ref, wu_ref, wd_ref)

    @pl.when(j > 0)
    def _():
        acc_ref[...] += _swiglu_step(h_ref[...], wg_ref, wu_ref, wd_ref)

    @pl.when(j == pl.num_programs(1) - 1)
    def _():
        x2 = x_ref[...] + acc_ref[...]
        x_out_ref[...] = x2
        h_out_ref[...] = _rms(x2, g_next_ref[...]).astype(jnp.bfloat16)


def _moe_ffn_body(te_ref, nv_ref, x_ref, g_ref, wg_ref, wu_ref, wd_ref, y_out_ref, h_ref, acc_ref):
    i = pl.program_id(0)
    j = pl.program_id(1)

    @pl.when(i < nv_ref[0])
    def _():
        @pl.when(j == 0)
        def _():
            h = _rms(x_ref[...], g_ref[...]).astype(jnp.bfloat16)
            h_ref[...] = h
            acc_ref[...] = _swiglu_step(h, wg_ref, wu_ref, wd_ref)

        @pl.when(j > 0)
        def _():
            acc_ref[...] += _swiglu_step(h_ref[...], wg_ref, wu_ref, wd_ref)

        @pl.when(j == pl.num_programs(1) - 1)
        def _():
            y_out_ref[...] = acc_ref[...]

    @pl.when((i >= nv_ref[0]) & (j == 0))
    def _():
        y_out_ref[...] = jnp.zeros_like(y_out_ref)


def _weight_specs(d, ff, tf, expert_of):
    wg_spec = pl.BlockSpec((None, d, tf), lambda i, j, *s: (expert_of(i, *s), 0, j))
    wd_spec = pl.BlockSpec((None, tf, d), lambda i, j, *s: (expert_of(i, *s), j, 0))
    return wg_spec, wg_spec, wd_spec


def _dense_ffn(x, h, wg, wu, wd, g_next):
    n, d = x.shape
    ff = wg.shape[-1]
    tm, tf = TM_FFN, TF_FFN
    assert n % tm == 0 and ff % tf == 0
    row_spec = pl.BlockSpec((tm, d), lambda i, j, *s: (i, 0))
    zero = jnp.zeros((1,), jnp.int32)
    return pl.pallas_call(
        _dense_ffn_body,
        grid_spec=pltpu.PrefetchScalarGridSpec(
            num_scalar_prefetch=1,
            grid=(n // tm, ff // tf),
            in_specs=[row_spec, row_spec, *_weight_specs(d, ff, tf, lambda i, te: te[0]),
                      pl.BlockSpec(g_next.shape, lambda i, j, *s: (0, 0))],
            out_specs=[row_spec, row_spec],
            scratch_shapes=[pltpu.VMEM((tm, d), jnp.float32)],
        ),
        out_shape=[jax.ShapeDtypeStruct((n, d), jnp.float32), jax.ShapeDtypeStruct((n, d), jnp.bfloat16)],
        compiler_params=_cparams(("arbitrary", "arbitrary")),
        name="dense_ffn",
    )(zero, x, h, wg, wu, wd, g_next)


def _moe_ffn(tile_expert, n_valid, xs, g, wg, wu, wd):
    r, d = xs.shape
    ff = wg.shape[-1]
    tm, tf = TM_MOE, TF_FFN
    assert r % tm == 0 and ff % tf == 0
    n_tiles = r // tm

    def row_idx(i, j, te, nv):
        return (jnp.minimum(i, nv[0] - 1), 0)

    def expert_of(i, te, nv):
        return te[jnp.minimum(i, nv[0] - 1)]

    def ff_idx(i, j, nv):
        return jnp.where(i < nv[0], j, ff // tf - 1)

    wg_spec = pl.BlockSpec((None, d, tf), lambda i, j, te, nv: (expert_of(i, te, nv), 0, ff_idx(i, j, nv)))
    wd_spec = pl.BlockSpec((None, tf, d), lambda i, j, te, nv: (expert_of(i, te, nv), ff_idx(i, j, nv), 0))
    row_spec = pl.BlockSpec((tm, d), row_idx)
    return pl.pallas_call(
        _moe_ffn_body,
        grid_spec=pltpu.PrefetchScalarGridSpec(
            num_scalar_prefetch=2,
            grid=(n_tiles, ff // tf),
            in_specs=[row_spec, pl.BlockSpec(g.shape, lambda i, j, *s: (0, 0)), wg_spec, wg_spec, wd_spec],
            out_specs=pl.BlockSpec((tm, d), lambda i, j, *s: (i, 0)),
            scratch_shapes=[pltpu.VMEM((tm, d), jnp.bfloat16), pltpu.VMEM((tm, d), jnp.float32)],
        ),
        out_shape=jax.ShapeDtypeStruct((r, d), jnp.float32),
        compiler_params=_cparams(("arbitrary", "arbitrary")),
        name="moe_ffn",
    )(tile_expert, n_valid, xs, g, wg, wu, wd)


def _proj_rope_body(scale, h_ref, w_ref, pos_ref, invf_ref, sign_ref, o_ref):
    half = DIFF_HEAD_DIM // 2
    ang = pos_ref[...].astype(jnp.float32) * invf_ref[...]
    cos = jnp.cos(ang) * scale
    sin = jnp.sin(ang) * sign_ref[...] * scale
    lane = lax.broadcasted_iota(jnp.int32, (1, LANES), 1)
    first_half = (lane % DIFF_HEAD_DIM) < half
    rows = h_ref.shape[0] // PROJ_SPLIT
    for part in range(PROJ_SPLIT):
        rs = slice(part * rows, (part + 1) * rows)
        t = _bdot(h_ref[rs, :], w_ref[...])
        for hd in range(o_ref.shape[0]):
            blk = t[:, hd * HEAD_WIDTH:(hd + 1) * HEAD_WIDTH]
            partner = jnp.where(first_half, pltpu.roll(blk, LANES - half, axis=1), pltpu.roll(blk, half, axis=1))
            o_ref[hd, rs, :] = (blk * cos[rs] + partner * sin[rs]).astype(o_ref.dtype)


def _proj_rope(h, w, pos, invf, sign, batch, scale):
    n, d = h.shape
    seq = n // batch
    tm = TM_PROJ
    heads = d // HEAD_WIDTH
    tiles = seq // tm
    return pl.pallas_call(
        functools.partial(_proj_rope_body, scale),
        grid=(n // tm,),
        in_specs=[pl.BlockSpec((tm, d), lambda i: (i, 0)), _const_spec(w.shape),
                  pl.BlockSpec((tm, 1), lambda i: (i, 0)), _const_spec(invf.shape), _const_spec(sign.shape)],
        out_specs=pl.BlockSpec((None, heads, tm, HEAD_WIDTH), lambda i: (i // tiles, 0, i % tiles, 0)),
        out_shape=jax.ShapeDtypeStruct((batch, heads, seq, HEAD_WIDTH), jnp.bfloat16),
        compiler_params=_cparams(("parallel",)),
        name="proj_rope",
    )(h, w, pos, invf, sign)


def _proj_vt_body(h_ref, w_ref, o_ref):
    t = _bdot(h_ref[...], w_ref[...])
    for hd in range(o_ref.shape[0]):
        o_ref[hd, 0] = t[:, hd * HEAD_WIDTH:(hd + 1) * HEAD_WIDTH].T.astype(o_ref.dtype)


def _proj_vt(h, w, batch):
    n, d = h.shape
    seq = n // batch
    tm = TK_ATT
    heads = d // HEAD_WIDTH
    tiles = seq // tm
    return pl.pallas_call(
        _proj_vt_body,
        grid=(n // tm,),
        in_specs=[pl.BlockSpec((tm, d), lambda i: (i, 0)), _const_spec(w.shape)],
        out_specs=pl.BlockSpec((None, heads, 1, HEAD_WIDTH, tm), lambda i: (i // tiles, 0, i % tiles, 0, 0)),
        out_shape=jax.ShapeDtypeStruct((batch, heads, tiles, HEAD_WIDTH, tm), jnp.bfloat16),
        compiler_params=_cparams(("parallel",)),
        name="proj_vt",
    )(h, w)


def _attn_body(lam_init, lam_ref, q_ref, k_ref, vt_ref, g_ref, o_ref, s_ref, cmax_ref, m_ref, l_ref, acc_ref, bias_ref):
    tq = q_ref.shape[0]
    tk = vt_ref.shape[2]
    qi = pl.program_id(2)
    lane = lax.broadcasted_iota(jnp.int32, (1, HEAD_WIDTH), 1)
    q = q_ref[...]
    zero = jnp.zeros_like(q)
    qz = (jnp.where(lane < DIFF_HEAD_DIM, q, zero), jnp.where(lane >= DIFF_HEAD_DIM, q, zero))

    m_ref[...] = jnp.full_like(m_ref, NEG_BIG)
    l_ref[...] = jnp.zeros_like(l_ref)
    acc_ref[...] = jnp.zeros_like(acc_ref)

    @pl.when(qi == 0)
    def _():
        krow = lax.broadcasted_iota(jnp.int32, (tk, tq), 0)
        qcol = lax.broadcasted_iota(jnp.int32, (tk, tq), 1)
        bias_ref[...] = jnp.where(krow <= qcol, 0.0, NEG_BIG)

    def scores(kv, slot, masked, c):
        start = pl.multiple_of(kv * tk, tk)
        kb = k_ref[pl.ds(start, tk), :]
        s = lax.dot_general(kb, qz[c], (((1,), (1,)), ((), ())), preferred_element_type=jnp.float32)
        if masked:
            s = s + bias_ref[...]
        s_ref[slot, c] = s
        cmax_ref[slot, c] = jnp.max(s, axis=0, keepdims=True)

    def consume(kv, slot, c):
        m_old = m_ref[c]
        m_new = jnp.maximum(m_old, cmax_ref[slot, c])
        alpha = jnp.exp2(m_old - m_new)
        p = jnp.exp2(s_ref[slot, c] - m_new)
        l_ref[c] = alpha * l_ref[c] + jnp.sum(p, axis=0, keepdims=True)
        acc_ref[c] = alpha * acc_ref[c] + _bdot(vt_ref[kv], p.astype(jnp.bfloat16))
        m_ref[c] = m_new

    for c in range(2):
        scores(qi, 0, True, c)

    def step(j, read_slot):
        kv_cur = jnp.where(j == 0, qi, j - 1)
        for c in range(2):
            scores(j, 1 - read_slot, False, c)
            consume(kv_cur, read_slot, c)

    done = 0
    for group in ATT_UNROLLS:
        def run(t, carry, group=group, done=done):
            for u in range(group):
                step(done + group * t + u, u % 2)
            return carry

        trips = (qi - done) // group
        lax.fori_loop(0, trips, run, 0)
        done = done + trips * group
    last = jnp.where(qi == 0, qi, qi - 1)

    @pl.when(qi % 2 == 1)
    def _():
        step(qi - 1, 0)
        for c in range(2):
            consume(last, 1, c)

    @pl.when(qi % 2 == 0)
    def _():
        for c in range(2):
            consume(last, 0, c)

    o = acc_ref[0] / l_ref[0] - lam_ref[0] * (acc_ref[1] / l_ref[1])
    o = o * lax.rsqrt(jnp.mean(o * o, axis=0, keepdims=True) + EPS) * g_ref[...] * (1.0 - lam_init)
    o_ref[...] = o.T.astype(o_ref.dtype)


def _attention(lam, q, k, vt, g_col, lam_init):
    batch, heads, seq, hw = q.shape
    tq, tk = TQ_ATT, TK_ATT
    assert tq == tk and seq % tq == 0
    return pl.pallas_call(
        functools.partial(_attn_body, lam_init),
        grid=(batch, heads, seq // tq),
        in_specs=[pl.BlockSpec(memory_space=pltpu.SMEM),
                  pl.BlockSpec((None, None, tq, hw), lambda b, h, i: (b, h, i, 0)),
                  pl.BlockSpec((None, None, seq, hw), lambda b, h, i: (b, h, 0, 0)),
                  pl.BlockSpec((None, None, seq // tk, hw, tk), lambda b, h, i: (b, h, 0, 0, 0)),
                  pl.BlockSpec(g_col.shape, lambda b, h, i: (0, 0))],
        out_specs=pl.BlockSpec((None, None, tq, hw), lambda b, h, i: (b, h, i, 0)),
        out_shape=jax.ShapeDtypeStruct((batch, heads, seq, hw), jnp.bfloat16),
        scratch_shapes=[pltpu.VMEM((2, 2, tk, tq), jnp.float32), pltpu.VMEM((2, 2, 1, tq), jnp.float32),
                        pltpu.VMEM((2, 1, tq), jnp.float32), pltpu.VMEM((2, 1, tq), jnp.float32),
                        pltpu.VMEM((2, hw, tq), jnp.float32), pltpu.VMEM((tk, tq), jnp.float32)],
        compiler_params=_cparams(("parallel", "parallel", "arbitrary")),
        name="diff_attn",
    )(lam, q, k, vt, g_col)


ROUTE_IDX, ROUTE_GATE, ROUTE_RANK = 0, 2, 4
ROUTE_ROWS = 8
EXPERT_ROWS = 16


def _out_proj_body(o_ref, x_ref, w_o_ref, x_out_ref):
    o = jnp.concatenate([o_ref[hd] for hd in range(o_ref.shape[0])], axis=-1)
    x_out_ref[...] = x_ref[...] + _bdot(o, w_o_ref[...])


def _out_proj(o, x, w_o):
    batch, heads, seq, hw = o.shape
    d = heads * hw
    tm = TM_OUT
    tiles = seq // tm
    row_spec = pl.BlockSpec((tm, d), lambda b, i: (b * tiles + i, 0))
    return pl.pallas_call(
        _out_proj_body,
        grid=(batch, tiles),
        in_specs=[pl.BlockSpec((None, heads, tm, hw), lambda b, i: (b, 0, i, 0)), row_spec, _const_spec(w_o.shape)],
        out_specs=row_spec,
        out_shape=jax.ShapeDtypeStruct(x.shape, jnp.float32),
        compiler_params=_cparams(("parallel", "parallel")),
        name="out_proj",
    )(o, x, w_o)


def _router_body(x_ref, g_ref, w_rt_ref, route_ref, count_ref, run_ref):
    tm = x_ref.shape[0]

    @pl.when(pl.program_id(0) == 0)
    def _():
        run_ref[...] = jnp.zeros_like(run_ref)

    h = _rms(x_ref[...], g_ref[...])
    h_hi = h.astype(jnp.bfloat16)
    h_lo = (h - h_hi.astype(jnp.float32)).astype(jnp.bfloat16)
    w_rt = w_rt_ref[...]
    w_hi = w_rt.astype(jnp.bfloat16)
    w_lo = (w_rt - w_hi.astype(jnp.float32)).astype(jnp.bfloat16)
    nt = (((1,), (1,)), ((), ()))
    logits = (lax.dot_general(w_hi, h_hi, nt, preferred_element_type=jnp.float32)
              + lax.dot_general(w_lo, h_hi, nt, preferred_element_type=jnp.float32)
              + lax.dot_general(w_hi, h_lo, nt, preferred_element_type=jnp.float32))
    row = lax.broadcasted_iota(jnp.int32, (EXPERT_ROWS, tm), 0).astype(jnp.float32)
    logits = jnp.where(row < N_EXPERTS, logits, -jnp.inf)
    m1 = jnp.max(logits, axis=0, keepdims=True)
    e1 = jnp.min(jnp.where(logits == m1, row, float(EXPERT_ROWS)), axis=0, keepdims=True)
    sel1 = row == e1
    rest = jnp.where(sel1, -jnp.inf, logits)
    m2 = jnp.max(rest, axis=0, keepdims=True)
    e2 = jnp.min(jnp.where(rest == m2, row, float(EXPERT_ROWS)), axis=0, keepdims=True)
    sel2 = row == e2
    t = jnp.exp(m2 - m1)
    w1 = 1.0 / (1.0 + t)
    w2 = t / (1.0 + t)

    onehot = (sel1 | sel2).astype(jnp.bfloat16)
    r_i = lax.broadcasted_iota(jnp.int32, (tm, tm), 0)
    c_i = lax.broadcasted_iota(jnp.int32, (tm, tm), 1)
    earlier = (r_i < c_i).astype(jnp.bfloat16)
    before = _bdot(onehot, earlier) + run_ref[:, 0:1]
    run_ref[...] += jnp.sum(onehot.astype(jnp.float32), axis=1, keepdims=True)
    rank1 = jnp.sum(jnp.where(sel1, before, 0.0), axis=0, keepdims=True)
    rank2 = jnp.sum(jnp.where(sel2, before, 0.0), axis=0, keepdims=True)

    rrow = lax.broadcasted_iota(jnp.int32, (ROUTE_ROWS, tm), 0)
    route = jnp.zeros((ROUTE_ROWS, tm), jnp.float32)
    for rw, val in ((ROUTE_IDX, e1), (ROUTE_IDX + 1, e2), (ROUTE_GATE, w1), (ROUTE_GATE + 1, w2),
                    (ROUTE_RANK, rank1), (ROUTE_RANK + 1, rank2)):
        route = jnp.where(rrow == rw, val, route)
    route_ref[...] = route
    count_ref[...] = run_ref[...]


def _router(x, g, w_rt):
    n, d = x.shape
    tm = TM_ROUTER
    return pl.pallas_call(
        _router_body,
        grid=(n // tm,),
        in_specs=[pl.BlockSpec((tm, d), lambda i: (i, 0)), _const_spec(g.shape), _const_spec(w_rt.shape)],
        out_specs=[pl.BlockSpec((ROUTE_ROWS, tm), lambda i: (0, i)),
                   pl.BlockSpec((EXPERT_ROWS, LANES), lambda i: (0, 0))],
        out_shape=[jax.ShapeDtypeStruct((ROUTE_ROWS, n), jnp.float32),
                   jax.ShapeDtypeStruct((EXPERT_ROWS, LANES), jnp.float32)],
        scratch_shapes=[pltpu.VMEM((EXPERT_ROWS, LANES), jnp.float32)],
        compiler_params=_cparams(("arbitrary",)),
        name="router",
    )(x, g, w_rt)


def _row_copy(src_ref, dst_ref, sem):
    return pltpu.make_async_copy(src_ref, dst_ref, sem)


def _dispatch_body(dest_ref, x_ref, xs_in_ref, xs_ref, sem):
    del xs_in_ref
    tm = x_ref.shape[0]

    def issue(r, c):
        for k in range(TOP_K):
            _row_copy(x_ref.at[pl.ds(r, 1)], xs_ref.at[pl.ds(dest_ref[0, k * tm + r], 1)], sem).start(priority=k)
        return c

    lax.fori_loop(0, tm, issue, 0, unroll=ROW_DMA_UNROLL)

    def drain(r, c):
        for k in range(TOP_K):
            _row_copy(x_ref.at[pl.ds(0, 1)], xs_ref.at[pl.ds(0, 1)], sem).wait()
        return c

    lax.fori_loop(0, tm, drain, 0, unroll=ROW_DMA_UNROLL)


def _dispatch(dest3, x, rows):
    n, d = x.shape
    tm = TM_ROW
    zeros = jnp.zeros((rows, d), x.dtype)
    return pl.pallas_call(
        _dispatch_body,
        grid=(n // tm,),
        in_specs=[pl.BlockSpec((None, 1, TOP_K * tm), lambda i: (i, 0, 0), memory_space=pltpu.SMEM),
                  pl.BlockSpec((tm, d), lambda i: (i, 0)),
                  pl.BlockSpec(memory_space=pl.ANY)],
        out_specs=pl.BlockSpec(memory_space=pl.ANY),
        out_shape=jax.ShapeDtypeStruct((rows, d), x.dtype),
        scratch_shapes=[pltpu.SemaphoreType.DMA],
        input_output_aliases={2: 0},
        compiler_params=_cparams(("arbitrary",)),
        name="dispatch",
    )(dest3, x, zeros)


def _combine_body(dest_ref, x_ref, gate_ref, g_ref, ys_ref, o_ref, buf_ref, sem):
    tm = x_ref.shape[0]

    def issue(r, c):
        for k in range(TOP_K):
            _row_copy(ys_ref.at[pl.ds(dest_ref[0, k * tm + r], 1)], buf_ref.at[k, pl.ds(r, 1)], sem).start(priority=k)
        return c

    lax.fori_loop(0, tm, issue, 0, unroll=ROW_DMA_UNROLL)

    def drain(r, c):
        for k in range(TOP_K):
            _row_copy(ys_ref.at[pl.ds(0, 1)], buf_ref.at[k, pl.ds(0, 1)], sem).wait()
        return c

    lax.fori_loop(0, tm, drain, 0, unroll=ROW_DMA_UNROLL)

    gates = gate_ref[...]
    y = x_ref[...] + gates[:, 0:1] * buf_ref[0] + gates[:, 1:2] * buf_ref[1]
    o_ref[...] = _rms(y, g_ref[...])


def _combine(dest3, x, gates, g, ys):
    n, d = x.shape
    tm = TM_ROW
    return pl.pallas_call(
        _combine_body,
        grid=(n // tm,),
        in_specs=[pl.BlockSpec((None, 1, TOP_K * tm), lambda i: (i, 0, 0), memory_space=pltpu.SMEM),
                  pl.BlockSpec((tm, d), lambda i: (i, 0)),
                  pl.BlockSpec((tm, TOP_K), lambda i: (i, 0)),
                  _const_spec(g.shape),
                  pl.BlockSpec(memory_space=pl.ANY)],
        out_specs=pl.BlockSpec((tm, d), lambda i: (i, 0)),
        out_shape=jax.ShapeDtypeStruct((n, d), jnp.float32),
        scratch_shapes=[pltpu.VMEM((TOP_K, tm, d), jnp.float32), pltpu.SemaphoreType.DMA],
        compiler_params=_cparams(("arbitrary",)),
        name="combine",
    )(dest3, x, gates, g, ys)


def _row(v):
    return v.reshape(1, -1).astype(jnp.float32)


def kernel(x, positions, ev_norm_mix, ev_w_in, ev_w_pool, ev_pool_scale, ev_ln_g, ev_ln_b, ev_w_spatial, ev_b_spatial, ev_w_out, ev_norm_ffn, ev_w_gate, ev_w_up, ev_w_down, od_norm_attn, od_w_qkv, od_lam_q1, od_lam_k1, od_lam_q2, od_lam_k2, od_subln_g, od_w_o, od_norm_moe, od_w_router, od_we_gate, od_we_up, od_we_down, final_norm):
    assert ev_norm_mix.shape[0] == 1 and od_norm_attn.shape[0] == 1, "one even and one odd layer"
    batch, seq, d = x.shape
    n = batch * seq
    bf16 = jnp.bfloat16
    gm_w = ev_ln_g.shape[-1]

    b_sp_full = jnp.repeat(ev_b_spatial[0].T, gm_w // GMLP_GROUPS, axis=1)
    x1, h1 = _mixer(x.reshape(n, d), seq, _row(ev_norm_mix[0]), ev_w_in[0].astype(bf16), ev_w_pool[0].astype(bf16),
                    _row(ev_pool_scale[0]), _row(ev_ln_g[0]), _row(ev_ln_b[0]), ev_w_spatial[0], b_sp_full,
                    ev_w_out[0].astype(bf16), _row(ev_norm_ffn[0]))
    x2, h2 = _dense_ffn(x1, h1, ev_w_gate.astype(bf16), ev_w_up.astype(bf16), ev_w_down.astype(bf16),
                        _row(od_norm_attn[0]))

    lam_init = 0.8 - 0.6 * math.exp(-0.3 * 1)
    lam = (jnp.exp(jnp.sum(od_lam_q1[0] * od_lam_k1[0])) - jnp.exp(jnp.sum(od_lam_q2[0] * od_lam_k2[0]))
           + lam_init).reshape(1).astype(jnp.float32)
    half = DIFF_HEAD_DIM // 2
    inv_freq = ROPE_THETA ** (-jnp.arange(0, DIFF_HEAD_DIM, 2, dtype=jnp.float32) / DIFF_HEAD_DIM)
    invf = jnp.tile(inv_freq, LANES // half).reshape(1, LANES)
    sign = jnp.tile(jnp.concatenate([-jnp.ones(half), jnp.ones(half)]), LANES // DIFF_HEAD_DIM)
    sign = sign.reshape(1, LANES).astype(jnp.float32)
    w_qkv = od_w_qkv[0].astype(bf16)
    pos = positions.reshape(n, 1)
    q = _proj_rope(h2, w_qkv[:, :d], pos, invf, sign, batch, DIFF_HEAD_DIM ** -0.5 * LOG2_E)
    k = _proj_rope(h2, w_qkv[:, d:2 * d], pos, invf, sign, batch, 1.0)
    vt = _proj_vt(h2, w_qkv[:, 2 * d:], batch)
    o = _attention(lam, q, k, vt, od_subln_g[0].reshape(-1, 1).astype(jnp.float32), lam_init)

    w_rt = jnp.zeros((EXPERT_ROWS, d), jnp.float32).at[:N_EXPERTS].set(od_w_router[0].T)
    x3 = _out_proj(o, x2, od_w_o[0].astype(bf16))
    route, counts = _router(x3, _row(od_norm_moe[0]), w_rt)

    tm_e = TM_MOE
    rows = (n * TOP_K // tm_e + N_EXPERTS) * tm_e
    cnt = counts[:N_EXPERTS, 0].astype(jnp.int32)
    padded = (cnt + tm_e - 1) // tm_e * tm_e
    ends = jnp.cumsum(padded)
    offs = ends - padded
    idx = route[ROUTE_IDX:ROUTE_IDX + TOP_K].astype(jnp.int32)
    rank = route[ROUTE_RANK:ROUTE_RANK + TOP_K].astype(jnp.int32)
    expert_ids = jnp.arange(N_EXPERTS, dtype=jnp.int32).reshape(N_EXPERTS, 1, 1)
    dest = rank + jnp.sum(jnp.where(idx[None] == expert_ids, offs.reshape(N_EXPERTS, 1, 1), 0), axis=0)
    gates = route[ROUTE_GATE:ROUTE_GATE + TOP_K].T
    tm_r = TM_ROW
    dest3 = dest.reshape(TOP_K, n // tm_r, tm_r).transpose(1, 0, 2).reshape(n // tm_r, 1, TOP_K * tm_r)
    n_valid = (ends[-1] // tm_e).reshape(1).astype(jnp.int32)
    tile_start = jnp.arange(rows // tm_e, dtype=jnp.int32) * tm_e
    tile_expert = jnp.minimum(jnp.sum(tile_start[:, None] >= ends[None, :], axis=1), N_EXPERTS - 1).astype(jnp.int32)

    xs = _dispatch(dest3, x3, rows)
    ys = _moe_ffn(tile_expert, n_valid, xs, _row(od_norm_moe[0]), od_we_gate[0].astype(bf16),
                  od_we_up[0].astype(bf16), od_we_down[0].astype(bf16))
    out = _combine(dest3, x3, gates, _row(final_norm), ys)
    return out.reshape(batch, seq, d)
```
